```python
import math
import functools
import jax
import jax.numpy as jnp
from jax import lax
import numpy as np

D_MODEL = 1024
BATCH = 2
SEQ = 8192
DEPTH = 4
DEC_BATCH = 128
DEC_SEQ = 8
PAST_LEN = 2048
PAGE_SIZE = 128

D_S5 = D_MODEL // 2
S5_GROUP = 16
S5_GROUPS = D_S5 // S5_GROUP
S5_STATE = 64
N_HEADS = 4
HEAD_DIM = 64
V_DIM = 2 * HEAD_DIM
D_QK = N_HEADS * 2 * HEAD_DIM
D_ATT = N_HEADS * V_DIM
D_FF = ((8 * D_MODEL // 3 + 127) // 128) * 128
SPLITS = (D_S5, D_S5 + D_QK, D_S5 + 2 * D_QK, D_S5 + 2 * D_QK + D_ATT,
          D_S5 + 2 * D_QK + D_ATT + D_MODEL)
D_IN = D_S5 + 2 * D_QK + D_ATT + 2 * D_MODEL
Q_BLOCK = 128
RMS_EPS = 1e-6
SUBLN_EPS = 1e-5

kernel_name = "hybrid_s5_diffattn_macaron_step"


def rms_norm(x, g, eps=RMS_EPS):
    xf = x.astype(jnp.float32)
    y = xf * lax.rsqrt(jnp.mean(xf * xf, axis=-1, keepdims=True) + eps)
    return (y * g.astype(jnp.float32)).astype(x.dtype)


def swiglu_half(x, g_pre, g_post, w_gate, w_up, w_down):
    h = rms_norm(x, g_pre)
    y = (jax.nn.silu(h @ w_gate) * (h @ w_up)) @ w_down
    return x + 0.5 * rms_norm(y, g_post)


def _cmul(ar, ai, br, bi):
    return ar * br - ai * bi, ar * bi + ai * br


def _s5_combine(e1, e2):
    a1r, a1i, b1r, b1i = e1
    a2r, a2i, b2r, b2i = e2
    ar, ai = _cmul(a2r, a2i, a1r, a1i)
    br, bi = _cmul(a2r, a2i, b1r, b1i)
    return ar, ai, br + b2r, bi + b2i


def s5_mixer(u, h0_re, h0_im, a_re, a_im, log_dt, b_re, b_im, c_re, c_im, d, w_glu):
    f32 = jnp.float32
    n, L, _ = u.shape
    uf = u.astype(f32)
    ug = uf.reshape(n, L, S5_GROUPS, S5_GROUP)
    a_re = a_re.astype(f32)
    a_im = a_im.astype(f32)
    dt = jnp.exp(log_dt.astype(f32))[:, None]
    mag = jnp.exp(dt * a_re)
    abar_re, abar_im = mag * jnp.cos(dt * a_im), mag * jnp.sin(dt * a_im)
    den = a_re * a_re + a_im * a_im
    fac_re, fac_im = _cmul(abar_re - 1.0, abar_im, a_re / den, -a_im / den)
    bu_re = jnp.einsum('nlgc,gpc->nlgp', ug, b_re.astype(f32))
    bu_im = jnp.einsum('nlgc,gpc->nlgp', ug, b_im.astype(f32))
    bb_re, bb_im = _cmul(fac_re, fac_im, bu_re, bu_im)
    ar = jnp.broadcast_to(abar_re, bb_re.shape)
    ai = jnp.broadcast_to(abar_im, bb_re.shape)
    acum_re, acum_im, hs_re, hs_im = lax.associative_scan(
        _s5_combine, (ar, ai, bb_re, bb_im), axis=1)
    ic_re, ic_im = _cmul(acum_re, acum_im,
                         h0_re.astype(f32)[:, None], h0_im.astype(f32)[:, None])
    h_re = hs_re + ic_re
    h_im = hs_im + ic_im
    y = (jnp.einsum('nlgp,gcp->nlgc', h_re, c_re.astype(f32))
         - jnp.einsum('nlgp,gcp->nlgc', h_im, c_im.astype(f32)))
    y = y.reshape(n, L, D_S5) + d.astype(f32) * uf
    z = jax.nn.gelu(y).astype(u.dtype)
    zg = z @ w_glu
    out = zg[..., :D_S5] * jax.nn.sigmoid(zg[..., D_S5:])
    return out, h_re[:, -1].astype(u.dtype), h_im[:, -1].astype(u.dtype)


def diff_attend(q, k, v, q_pos, k_pos, lam):
    s = jnp.einsum('nqhcd,nkhcd->nhcqk', q, k).astype(jnp.float32) * (HEAD_DIM ** -0.5)
    mask = k_pos[None, :] <= q_pos[:, None]
    s = jnp.where(mask, s, jnp.finfo(jnp.float32).min)
    p = jax.nn.softmax(s, axis=-1)
    pd = p[:, :, 0] - lam * p[:, :, 1]
    return jnp.einsum('nhqk,nkhe->nqhe', pd.astype(v.dtype), v)


def causal_diff_attention(q, k, v, lam):
    n, L = q.shape[:2]
    nb = L // Q_BLOCK
    qb = q.reshape(n, nb, Q_BLOCK, N_HEADS, 2, HEAD_DIM).swapaxes(0, 1)
    pos = jnp.arange(L, dtype=jnp.int32)

    def one_block(args):
        q_i, qpos_i = args
        return diff_attend(q_i, k, v, qpos_i, pos, lam)

    o = lax.map(one_block, (qb, pos.reshape(nb, Q_BLOCK)))
    return o.swapaxes(0, 1).reshape(n, L, N_HEADS, V_DIM)


def paged_diff_attention(q, k_new, v_new, lam, k_pool, v_pool, page_table):
    n, T = q.shape[:2]
    past = page_table.shape[1] * PAGE_SIZE
    k_past = k_pool[page_table].reshape(n, past, N_HEADS, 2, HEAD_DIM)
    v_past = v_pool[page_table].reshape(n, past, N_HEADS, V_DIM)
    k_all = jnp.concatenate([k_past, k_new], axis=1)
    v_all = jnp.concatenate([v_past, v_new], axis=1)
    k_pos = jnp.arange(past + T, dtype=jnp.int32)
    q_pos = past + jnp.arange(T, dtype=jnp.int32)
    return diff_attend(q, k_all, v_all, q_pos, k_pos, lam)


def token_mixer(x, attend, h0_re, h0_im, lambda_init, mw):
    n, L, _ = x.shape
    h = rms_norm(x, mw['g_pre'])
    proj = h @ mw['w_in']
    u, q, k, v, gate_s5, gate_att = jnp.split(proj, SPLITS, axis=-1)
    y_s5, s_re, s_im = s5_mixer(u, h0_re, h0_im, mw['a_re'], mw['a_im'], mw['log_dt'],
                                mw['b_re'], mw['b_im'], mw['c_re'], mw['c_im'],
                                mw['d'], mw['w_glu'])
    q = q.reshape(n, L, N_HEADS, 2, HEAD_DIM)
    k = k.reshape(n, L, N_HEADS, 2, HEAD_DIM)
    v = v.reshape(n, L, N_HEADS, V_DIM)
    f32 = jnp.float32
    lam = (jnp.exp(jnp.sum(mw['lq1'].astype(f32) * mw['lk1'].astype(f32)))
           - jnp.exp(jnp.sum(mw['lq2'].astype(f32) * mw['lk2'].astype(f32)))
           + lambda_init)
    o = attend(q, k, v, lam)
    o = rms_norm(o, mw['g_subln'], SUBLN_EPS) * (1.0 - lambda_init)
    y_att = o.reshape(n, L, D_ATT)
    merged = (jax.nn.sigmoid(gate_s5) * (y_s5 @ mw['w_b_s5'])
              + jax.nn.sigmoid(gate_att) * (y_att @ mw['w_b_att']))
    out = merged @ mw['w_out']
    return x + rms_norm(out, mw['g_post']), k, v, s_re, s_im


def setup_inputs(seed: int = 0) -> dict:
    key = jax.random.key(seed)
    ks = iter(jax.random.split(key, 48))
    f32 = jnp.float32

    def nrm(shape, scale=1.0):
        return jax.random.normal(next(ks), shape, f32) * scale

    def gain(width):
        return 1.0 + 0.05 * nrm((DEPTH, width))

    n_pages = PAST_LEN // PAGE_SIZE
    n_used = DEC_BATCH * n_pages
    n_phys = n_used + max(1, n_used // 4)

    inp = {}
    inp['x_prompt'] = nrm((BATCH, SEQ, D_MODEL))
    inp['x_sample'] = nrm((DEC_BATCH, DEC_SEQ, D_MODEL))
    inp['cache_k'] = nrm((DEPTH, n_phys, PAGE_SIZE, N_HEADS, 2, HEAD_DIM))
    inp['cache_v'] = nrm((DEPTH, n_phys, PAGE_SIZE, N_HEADS, V_DIM))
    inp['state_s5_re'] = nrm((DEPTH, DEC_BATCH, S5_GROUPS, S5_STATE), 0.5)
    inp['state_s5_im'] = nrm((DEPTH, DEC_BATCH, S5_GROUPS, S5_STATE), 0.5)
    inp['page_table'] = jax.random.permutation(next(ks), n_phys)[:n_used].reshape(
        DEC_BATCH, n_pages).astype(jnp.int32)
    inp['g_ffn1_pre'] = gain(D_MODEL)
    inp['g_ffn1_post'] = gain(D_MODEL)
    inp['w_ffn1_gate'] = nrm((DEPTH, D_MODEL, D_FF), D_MODEL ** -0.5)
    inp['w_ffn1_up'] = nrm((DEPTH, D_MODEL, D_FF), D_MODEL ** -0.5)
    inp['w_ffn1_down'] = nrm((DEPTH, D_FF, D_MODEL), D_FF ** -0.5)
    inp['g_mix_pre'] = gain(D_MODEL)
    inp['g_mix_post'] = gain(D_MODEL)
    inp['w_in'] = nrm((DEPTH, D_MODEL, D_IN), D_MODEL ** -0.5)
    inp['s5_a_re'] = -0.5 + 0.01 * nrm((DEPTH, S5_GROUPS, S5_STATE))
    inp['s5_a_im'] = (math.pi * jnp.arange(S5_STATE, dtype=f32))[None, None, :] \
        + 0.01 * nrm((DEPTH, S5_GROUPS, S5_STATE))
    inp['s5_log_dt'] = jax.random.uniform(next(ks), (DEPTH, S5_GROUPS), f32,
                                          math.log(1e-3), math.log(1e-1))
    inp['s5_b_re'] = nrm((DEPTH, S5_GROUPS, S5_STATE, S5_GROUP), (2 * S5_GROUP) ** -0.5)
    inp['s5_b_im'] = nrm((DEPTH, S5_GROUPS, S5_STATE, S5_GROUP), (2 * S5_GROUP) ** -0.5)
    inp['s5_c_re'] = nrm((DEPTH, S5_GROUPS, S5_GROUP, S5_STATE), (2 * S5_STATE) ** -0.5)
    inp['s5_c_im'] = nrm((DEPTH, S5_GROUPS, S5_GROUP, S5_STATE), (2 * S5_STATE) ** -0.5)
    inp['s5_d'] = nrm((DEPTH, D_S5))
    inp['w_glu'] = nrm((DEPTH, D_S5, 2 * D_S5), D_S5 ** -0.5)
    inp['lambda_q1'] = nrm((DEPTH, HEAD_DIM), 0.1)
    inp['lambda_k1'] = nrm((DEPTH, HEAD_DIM), 0.1)
    inp['lambda_q2'] = nrm((DEPTH, HEAD_DIM), 0.1)
    inp['lambda_k2'] = nrm((DEPTH, HEAD_DIM), 0.1)
    inp['g_subln'] = gain(V_DIM)
    inp['w_branch_s5'] = nrm((DEPTH, D_S5, D_MODEL), D_S5 ** -0.5)
    inp['w_branch_att'] = nrm((DEPTH, D_ATT, D_MODEL), D_ATT ** -0.5)
    inp['w_out'] = nrm((DEPTH, D_MODEL, D_MODEL), D_MODEL ** -0.5)
    inp['g_ffn2_pre'] = gain(D_MODEL)
    inp['g_ffn2_post'] = gain(D_MODEL)
    inp['w_ffn2_gate'] = nrm((DEPTH, D_MODEL, D_FF), D_MODEL ** -0.5)
    inp['w_ffn2_up'] = nrm((DEPTH, D_MODEL, D_FF), D_MODEL ** -0.5)
    inp['w_ffn2_down'] = nrm((DEPTH, D_FF, D_MODEL), D_FF ** -0.5)
    return inp


def reference(x_prompt, x_sample, cache_k, cache_v, state_s5_re, state_s5_im, page_table,
              g_ffn1_pre, g_ffn1_post, w_ffn1_gate, w_ffn1_up, w_ffn1_down,
              g_mix_pre, g_mix_post, w_in,
              s5_a_re, s5_a_im, s5_log_dt, s5_b_re, s5_b_im, s5_c_re, s5_c_im, s5_d, w_glu,
              lambda_q1, lambda_k1, lambda_q2, lambda_k2, g_subln,
              w_branch_s5, w_branch_att, w_out,
              g_ffn2_pre, g_ffn2_post, w_ffn2_gate, w_ffn2_up, w_ffn2_down):
    xp, xs = x_prompt, x_sample
    k_prompt_rows, v_prompt_rows, s5re_prompt, s5im_prompt = [], [], [], []
    k_sample_rows, v_sample_rows, s5re_sample, s5im_sample = [], [], [], []
    for l in range(DEPTH):
        lambda_init = 0.8 - 0.6 * math.exp(-0.3 * l)
        mw = {
            'g_pre': g_mix_pre[l], 'g_post': g_mix_post[l], 'w_in': w_in[l],
            'a_re': s5_a_re[l], 'a_im': s5_a_im[l], 'log_dt': s5_log_dt[l],
            'b_re': s5_b_re[l], 'b_im': s5_b_im[l], 'c_re': s5_c_re[l], 'c_im': s5_c_im[l],
            'd': s5_d[l], 'w_glu': w_glu[l],
            'lq1': lambda_q1[l], 'lk1': lambda_k1[l], 'lq2': lambda_q2[l], 'lk2': lambda_k2[l],
            'g_subln': g_subln[l], 'w_b_s5': w_branch_s5[l], 'w_b_att': w_branch_att[l],
            'w_out': w_out[l],
        }
        xp = swiglu_half(xp, g_ffn1_pre[l], g_ffn1_post[l], w_ffn1_gate[l], w_ffn1_up[l], w_ffn1_down[l])
        xs = swiglu_half(xs, g_ffn1_pre[l], g_ffn1_post[l], w_ffn1_gate[l], w_ffn1_up[l], w_ffn1_down[l])
        zero_state = jnp.zeros((xp.shape[0], S5_GROUPS, S5_STATE), xp.dtype)
        xp, kp, vp, sp_re, sp_im = token_mixer(xp, causal_diff_attention, zero_state, zero_state,
                                               lambda_init, mw)
        attend_sample = functools.partial(paged_diff_attention, k_pool=cache_k[l],
                                          v_pool=cache_v[l], page_table=page_table)
        xs, ksn, vsn, ss_re, ss_im = token_mixer(xs, attend_sample, state_s5_re[l], state_s5_im[l],
                                                 lambda_init, mw)
        xp = swiglu_half(xp, g_ffn2_pre[l], g_ffn2_post[l], w_ffn2_gate[l], w_ffn2_up[l], w_ffn2_down[l])
        xs = swiglu_half(xs, g_ffn2_pre[l], g_ffn2_post[l], w_ffn2_gate[l], w_ffn2_up[l], w_ffn2_down[l])
        k_prompt_rows.append(kp)
        v_prompt_rows.append(vp)
        s5re_prompt.append(sp_re)
        s5im_prompt.append(sp_im)
        k_sample_rows.append(ksn)
        v_sample_rows.append(vsn)
        s5re_sample.append(ss_re)
        s5im_sample.append(ss_im)
    return (xp, xs,
            jnp.stack(k_prompt_rows), jnp.stack(v_prompt_rows),
            jnp.stack(s5re_prompt), jnp.stack(s5im_prompt),
            jnp.stack(k_sample_rows), jnp.stack(v_sample_rows),
            jnp.stack(s5re_sample), jnp.stack(s5im_sample))
```

```python
import functools
import math

import jax
import jax.numpy as jnp
from jax import lax
from jax.experimental import pallas as pl
from jax.experimental.pallas import tpu as pltpu

F32 = jnp.float32
BF16 = jnp.bfloat16

PAGE_SIZE = 128
DEC_SEQ = 8
S5_GROUP = 16
S5_STATE = 64
N_HEADS = 4
HEAD_DIM = 64
V_DIM = 2 * HEAD_DIM
RMS_EPS = 1e-6
SUBLN_EPS = 1e-5
NEG_BIG = -1e30

LANES = 128
SUBLANES = 8
GROUPS_PER_SLAB = LANES // S5_GROUP
SLAB_STATES = GROUPS_PER_SLAB * S5_STATE
FF_CHUNK = 256
VMEM_LIMIT = 56 * 1024 * 1024


def _largest_tile(n, pref):
    t = pref
    while n % t:
        t //= 2
    return t


def _rms(x, g, eps):
    return x * lax.rsqrt(jnp.mean(x * x, axis=-1, keepdims=True) + eps) * g


def _cparams(*sem):
    return pltpu.CompilerParams(dimension_semantics=sem, vmem_limit_bytes=VMEM_LIMIT)


def _resident(shape):
    nd = len(shape)
    return pl.BlockSpec(shape, lambda *_: (0,) * nd, pipeline_mode=pl.Buffered(1))


def _ffn_kernel(x_ref, gpre_ref, gpost_ref, wgu_ref, wd_ref, o_ref, h_scr, acc_scr):
    x = x_ref[...]
    h_scr[...] = _rms(x, gpre_ref[...], RMS_EPS).astype(BF16)
    n_chunks = wgu_ref.shape[0]

    def chunk(c):
        gu = jnp.dot(h_scr[...], wgu_ref[c], preferred_element_type=F32)
        g = gu[:, :FF_CHUNK]
        a = (g * jax.nn.sigmoid(g) * gu[:, FF_CHUNK:]).astype(BF16)
        return jnp.dot(a, wd_ref[c], preferred_element_type=F32)

    acc_scr[...] = chunk(0)

    def body(c, carry):
        acc_scr[...] += chunk(c)
        return carry

    lax.fori_loop(1, n_chunks, body, 0)
    o_ref[...] = x + 0.5 * _rms(acc_scr[...], gpost_ref[...], RMS_EPS)


def _ffn(x, gpre, gpost, wgu, wd):
    t, d = x.shape
    tm = _largest_tile(t, 512)
    row = pl.BlockSpec((tm, d), lambda i: (i, 0))
    vec = pl.BlockSpec((1, d), lambda i: (0, 0))
    return pl.pallas_call(
        _ffn_kernel,
        out_shape=jax.ShapeDtypeStruct((t, d), F32),
        grid=(t // tm,),
        in_specs=[row, vec, vec, _resident(wgu.shape), _resident(wd.shape)],
        out_specs=row,
        scratch_shapes=[pltpu.VMEM((tm, d), BF16), pltpu.VMEM((tm, d), F32)],
        compiler_params=_cparams("arbitrary"),
        name="ffn",
    )(x, gpre, gpost, wgu, wd)


def _inproj_kernel(x_ref, g_ref, w_ref, u_ref, q_ref, kf_ref, kb_ref, vf_ref, vb_ref, gate_ref,
                   *, d_s5, d_qk, d_att):
    h = _rms(x_ref[...], g_ref[...], RMS_EPS).astype(BF16)

    def proj(lo, width):
        return jnp.dot(h, w_ref[:, lo:lo + width], preferred_element_type=F32)

    u = proj(0, d_s5)
    for j in range(d_s5 // LANES):
        u_ref[j] = u[:, j * LANES:(j + 1) * LANES]
    q_ref[...] = (proj(d_s5, d_qk) * (HEAD_DIM ** -0.5)).astype(BF16)
    k = proj(d_s5 + d_qk, d_qk)
    kf_ref[...] = k
    kb_ref[...] = k.astype(BF16)
    v = proj(d_s5 + 2 * d_qk, d_att)
    vf_ref[...] = v
    vb_ref[...] = v.astype(BF16)
    lo = d_s5 + 2 * d_qk + d_att
    gate_ref[...] = proj(lo, w_ref.shape[1] - lo)


def _inproj(x, g, w, d_s5, d_qk, d_att):
    t, d = x.shape
    tm = _largest_tile(t, 512)
    d_gate = w.shape[1] - d_s5 - 2 * d_qk - d_att

    def rows(width):
        return pl.BlockSpec((tm, width), lambda i: (i, 0))

    n_slab = d_s5 // LANES
    return pl.pallas_call(
        functools.partial(_inproj_kernel, d_s5=d_s5, d_qk=d_qk, d_att=d_att),
        out_shape=(jax.ShapeDtypeStruct((n_slab, t, LANES), F32),
                   jax.ShapeDtypeStruct((t, d_qk), BF16),
                   jax.ShapeDtypeStruct((t, d_qk), F32),
                   jax.ShapeDtypeStruct((t, d_qk), BF16),
                   jax.ShapeDtypeStruct((t, d_att), F32),
                   jax.ShapeDtypeStruct((t, d_att), BF16),
                   jax.ShapeDtypeStruct((t, d_gate), F32)),
        grid=(t // tm,),
        in_specs=[rows(d), pl.BlockSpec((1, d), lambda i: (0, 0)), _resident(w.shape)],
        out_specs=(pl.BlockSpec((n_slab, tm, LANES), lambda i: (0, i, 0)),
                   rows(d_qk), rows(d_qk), rows(d_qk), rows(d_att), rows(d_att), rows(d_gate)),
        compiler_params=_cparams("arbitrary"),
        name="in_proj",
    )(x, g, w)


def _s5_readout(st_scr, up_scr, cbd_ref, d_ref, wglu_ref, n_slab):
    ys = []
    for j in range(n_slab):
        hj = st_scr[:, 2 * SLAB_STATES * j:2 * SLAB_STATES * (j + 1)].astype(BF16)
        ys.append(jnp.dot(hj, cbd_ref[j], preferred_element_type=F32))
    y = jnp.concatenate(ys, axis=-1) + d_ref[...] * up_scr[...]
    z = jax.nn.gelu(y).astype(BF16)
    zg = jnp.dot(z, wglu_ref[...], preferred_element_type=F32)
    half = zg.shape[1] // 2
    return zg[:, :half] * jax.nn.sigmoid(zg[:, half:])


def _s5_input(st_scr, up_scr, bbd_ref, n_slab):
    for j in range(n_slab):
        uj = up_scr[:, j * LANES:(j + 1) * LANES].astype(BF16)
        st_scr[:, 2 * SLAB_STATES * j:2 * SLAB_STATES * (j + 1)] = jnp.dot(
            uj, bbd_ref[j], preferred_element_type=F32)


def _s5_prompt_kernel(u_ref, bbd_ref, cbd_ref, ab_ref, aps_ref, pw_ref, d_ref, wglu_ref,
                      y_ref, sre_ref, sim_ref, up_scr, st_scr, carry_scr, *, tc):
    c = pl.program_id(1)
    n_slab = u_ref.shape[0]
    seg = tc // SUBLANES
    w = SLAB_STATES

    @pl.when(c == 0)
    def _():
        carry_scr[...] = jnp.zeros_like(carry_scr)

    for j in range(n_slab):
        for k in range(seg):
            up_scr[k * SUBLANES:(k + 1) * SUBLANES, j * LANES:(j + 1) * LANES] = (
                u_ref[j, pl.ds(k, SUBLANES, stride=seg), :])
    _s5_input(st_scr, up_scr, bbd_ref, n_slab)

    row_id = lax.broadcasted_iota(jnp.int32, (SUBLANES, w), 0)
    for j in range(n_slab):
        re = slice(2 * w * j, 2 * w * j + w)
        im = slice(2 * w * j + w, 2 * w * (j + 1))
        a_re = ab_ref[j, 0]
        a_im = ab_ref[j, 1]

        def scan_step(k, h, re=re, im=im, a_re=a_re, a_im=a_im):
            h_re, h_im = h
            r = pl.ds(pl.multiple_of(k * SUBLANES, SUBLANES), SUBLANES)
            n_re = a_re * h_re - a_im * h_im + st_scr[r, re]
            n_im = a_re * h_im + a_im * h_re + st_scr[r, im]
            st_scr[r, re] = n_re
            st_scr[r, im] = n_im
            return n_re, n_im

        zero = jnp.zeros((SUBLANES, w), F32)
        e_re, e_im = lax.fori_loop(0, seg, scan_step, (zero, zero), unroll=2)

        s_re = aps_ref[j, 0][0:1]
        s_im = aps_ref[j, 1][0:1]
        cur_re = carry_scr[0:1, re]
        cur_im = carry_scr[0:1, im]
        c_re = jnp.broadcast_to(cur_re, (SUBLANES, w))
        c_im = jnp.broadcast_to(cur_im, (SUBLANES, w))
        for i in range(1, SUBLANES + 1):
            nxt_re = s_re * cur_re - s_im * cur_im + e_re[i - 1:i]
            nxt_im = s_re * cur_im + s_im * cur_re + e_im[i - 1:i]
            cur_re, cur_im = nxt_re, nxt_im
            if i < SUBLANES:
                c_re = jnp.where(row_id == i, cur_re, c_re)
                c_im = jnp.where(row_id == i, cur_im, c_im)
        carry_scr[0:1, re] = cur_re
        carry_scr[0:1, im] = cur_im

        def fix_step(k, carry, re=re, im=im, c_re=c_re, c_im=c_im, j=j):
            r = pl.ds(pl.multiple_of(k * SUBLANES, SUBLANES), SUBLANES)
            p_re = pw_ref[j, 0, k]
            p_im = pw_ref[j, 1, k]
            st_scr[r, re] += p_re * c_re - p_im * c_im
            st_scr[r, im] += p_re * c_im + p_im * c_re
            return carry

        lax.fori_loop(0, seg, fix_step, 0, unroll=2)

    up_scr[...] = _s5_readout(st_scr, up_scr, cbd_ref, d_ref, wglu_ref, n_slab)
    for j in range(n_slab):
        for k in range(seg):
            y_ref[j, pl.ds(k, SUBLANES, stride=seg), :] = (
                up_scr[k * SUBLANES:(k + 1) * SUBLANES, j * LANES:(j + 1) * LANES])

    @pl.when(c == pl.num_programs(1) - 1)
    def _():
        for j in range(n_slab):
            sre_ref[0, :, j * w:(j + 1) * w] = carry_scr[0:1, 2 * w * j:2 * w * j + w]
            sim_ref[0, :, j * w:(j + 1) * w] = carry_scr[0:1, 2 * w * j + w:2 * w * (j + 1)]


def _s5_prompt(u_slab, n_batch, seq, p):
    n_slab = u_slab.shape[0]
    d_s5 = n_slab * LANES
    tc = p["tc"]
    n_chunks = seq // tc
    n_state = n_slab * SLAB_STATES
    u_spec = pl.BlockSpec((n_slab, tc, LANES), lambda b, c: (0, b * n_chunks + c, 0))
    st_spec = pl.BlockSpec((1, 1, n_state), lambda b, c: (b, 0, 0))
    return pl.pallas_call(
        functools.partial(_s5_prompt_kernel, tc=tc),
        out_shape=(jax.ShapeDtypeStruct((n_slab, n_batch * seq, LANES), F32),
                   jax.ShapeDtypeStruct((n_batch, 1, n_state), F32),
                   jax.ShapeDtypeStruct((n_batch, 1, n_state), F32)),
        grid=(n_batch, n_chunks),
        in_specs=[u_spec, _resident(p["bbd"].shape), _resident(p["cbd"].shape),
                  _resident(p["ab"].shape), _resident(p["aps"].shape), _resident(p["pw"].shape),
                  _resident(p["d"].shape), _resident(p["wglu"].shape)],
        out_specs=(u_spec, st_spec, st_spec),
        scratch_shapes=[pltpu.VMEM((tc, d_s5), F32),
                        pltpu.VMEM((tc, 2 * n_state), F32),
                        pltpu.VMEM((SUBLANES, 2 * n_state), F32)],
        compiler_params=_cparams("arbitrary", "arbitrary"),
        name="s5_prompt",
    )(u_slab, p["bbd"], p["cbd"], p["ab"], p["aps"], p["pw"], p["d"], p["wglu"])


def _s5_sample_kernel(u_ref, h0re_ref, h0im_ref, bbd_ref, cbd_ref, ab_ref, d_ref, wglu_ref,
                      y_ref, sre_ref, sim_ref, up_scr, st_scr):
    n_slab = u_ref.shape[0]
    n_seq = h0re_ref.shape[0]
    w = SLAB_STATES
    for j in range(n_slab):
        for t in range(DEC_SEQ):
            up_scr[t * n_seq:(t + 1) * n_seq, j * LANES:(j + 1) * LANES] = (
                u_ref[j, pl.ds(t, n_seq, stride=DEC_SEQ), :])
    _s5_input(st_scr, up_scr, bbd_ref, n_slab)

    for j in range(n_slab):
        re = slice(2 * w * j, 2 * w * j + w)
        im = slice(2 * w * j + w, 2 * w * (j + 1))
        a_re = ab_ref[j, 0]
        a_im = ab_ref[j, 1]

        def seq_group(g, carry, re=re, im=im, a_re=a_re, a_im=a_im, j=j):
            n0 = pl.multiple_of(g * SUBLANES, SUBLANES)
            h_re = h0re_ref[pl.ds(n0, SUBLANES), j * w:(j + 1) * w]
            h_im = h0im_ref[pl.ds(n0, SUBLANES), j * w:(j + 1) * w]
            for t in range(DEC_SEQ):
                r = pl.ds(pl.multiple_of(t * n_seq + n0, SUBLANES), SUBLANES)
                n_re = a_re * h_re - a_im * h_im + st_scr[r, re]
                n_im = a_re * h_im + a_im * h_re + st_scr[r, im]
                st_scr[r, re] = n_re
                st_scr[r, im] = n_im
                h_re, h_im = n_re, n_im
            sre_ref[pl.ds(n0, SUBLANES), j * w:(j + 1) * w] = h_re
            sim_ref[pl.ds(n0, SUBLANES), j * w:(j + 1) * w] = h_im
            return carry

        lax.fori_loop(0, n_seq // SUBLANES, seq_group, 0)

    up_scr[...] = _s5_readout(st_scr, up_scr, cbd_ref, d_ref, wglu_ref, n_slab)
    for j in range(n_slab):
        for t in range(DEC_SEQ):
            y_ref[j, pl.ds(t, n_seq, stride=DEC_SEQ), :] = (
                up_scr[t * n_seq:(t + 1) * n_seq, j * LANES:(j + 1) * LANES])


def _s5_sample(u_slab, h0_re, h0_im, p):
    n_slab, rows, _ = u_slab.shape
    n_seq, n_state = h0_re.shape
    args = (u_slab, h0_re, h0_im, p["bbd"], p["cbd"], p["ab"], p["d"], p["wglu"])
    return pl.pallas_call(
        _s5_sample_kernel,
        out_shape=(jax.ShapeDtypeStruct(u_slab.shape, F32),
                   jax.ShapeDtypeStruct((n_seq, n_state), F32),
                   jax.ShapeDtypeStruct((n_seq, n_state), F32)),
        grid=(1,),
        in_specs=[_resident(a.shape) for a in args],
        out_specs=(_resident(u_slab.shape), _resident(h0_re.shape), _resident(h0_re.shape)),
        scratch_shapes=[pltpu.VMEM((rows, n_slab * LANES), F32),
                        pltpu.VMEM((rows, 2 * n_state), F32)],
        compiler_params=_cparams("arbitrary"),
        name="s5_sample",
    )(*args)


def _subln(o, g_ref, post_scale):
    return _rms(o, g_ref[...], SUBLN_EPS) * post_scale


def _attn_prompt_kernel(lam_ref, q_ref, k_ref, v_ref, g_ref, o_ref, q2_scr, m_scr, l_scr, acc_scr,
                        *, blk, post_scale):
    qi = pl.program_id(2)
    q = q_ref[...]
    lane = lax.broadcasted_iota(jnp.int32, q.shape, 1)
    zero = jnp.zeros_like(q)
    q2_scr[0:blk] = jnp.where(lane < HEAD_DIM, q, zero)
    q2_scr[blk:2 * blk] = jnp.where(lane >= HEAD_DIM, q, zero)
    m_scr[...] = jnp.full_like(m_scr, NEG_BIG)
    l_scr[...] = jnp.zeros_like(l_scr)
    acc_scr[...] = jnp.zeros_like(acc_scr)

    def block(kj, masked):
        r = pl.ds(pl.multiple_of(kj * blk, blk), blk)
        s = lax.dot_general(q2_scr[...], k_ref[r, :], (((1,), (1,)), ((), ())),
                            preferred_element_type=F32)
        if masked:
            qpos = lax.broadcasted_iota(jnp.int32, s.shape, 0) % blk
            kpos = lax.broadcasted_iota(jnp.int32, s.shape, 1)
            s = jnp.where(kpos <= qpos, s, NEG_BIG)
        m_old = m_scr[...]
        m_new = jnp.maximum(m_old, jnp.max(s, axis=-1, keepdims=True))
        alpha = jnp.exp(m_old - m_new)
        p = jnp.exp(s - m_new)
        l_scr[...] = alpha * l_scr[...] + jnp.sum(p, axis=-1, keepdims=True)
        acc_scr[...] = alpha * acc_scr[...] + jnp.dot(p.astype(BF16), v_ref[r, :],
                                                      preferred_element_type=F32)
        m_scr[...] = m_new

    def full_block(kj, carry):
        block(kj, False)
        return carry

    lax.fori_loop(0, qi, full_block, 0)
    block(qi, True)

    o1 = acc_scr[0:blk] / l_scr[0:blk]
    o2 = acc_scr[blk:2 * blk] / l_scr[blk:2 * blk]
    o_ref[...] = _subln(o1 - lam_ref[0, 0] * o2, g_ref, post_scale)


def _attn_prompt(lam, q, k, v, g_subln, n_batch, seq, post_scale):
    blk = _largest_tile(seq, 256)
    nq = seq // blk
    q_spec = pl.BlockSpec((blk, V_DIM), lambda b, h, i: (b * nq + i, h))
    kv_spec = pl.BlockSpec((seq, V_DIM), lambda b, h, i: (b, h))
    return pl.pallas_call(
        functools.partial(_attn_prompt_kernel, blk=blk, post_scale=post_scale),
        out_shape=jax.ShapeDtypeStruct((n_batch * seq, N_HEADS * V_DIM), F32),
        grid=(n_batch, N_HEADS, nq),
        in_specs=[pl.BlockSpec(memory_space=pltpu.SMEM), q_spec, kv_spec, kv_spec,
                  pl.BlockSpec((1, V_DIM), lambda b, h, i: (0, 0))],
        out_specs=q_spec,
        scratch_shapes=[pltpu.VMEM((2 * blk, V_DIM), BF16),
                        pltpu.VMEM((2 * blk, 1), F32),
                        pltpu.VMEM((2 * blk, 1), F32),
                        pltpu.VMEM((2 * blk, V_DIM), F32)],
        compiler_params=_cparams("arbitrary", "arbitrary", "arbitrary"),
        name="attn_prompt",
    )(lam, q, k, v, g_subln)


def _attn_sample_kernel(pt_ref, lam_ref, q_ref, kn_ref, vn_ref, g_ref, *rest, n_pages, post_scale):
    k_pages = rest[:n_pages]
    v_pages = rest[n_pages:2 * n_pages]
    o_ref = rest[2 * n_pages]
    k_scr, v_scr, s_scr, p_scr = rest[2 * n_pages + 1:]
    past = n_pages * PAGE_SIZE
    tail = 2 * DEC_SEQ
    n_col = N_HEADS * 2 * DEC_SEQ

    for j in range(n_pages):
        k_scr[j * PAGE_SIZE:(j + 1) * PAGE_SIZE] = k_pages[j][...].astype(BF16)
        v_scr[j * PAGE_SIZE:(j + 1) * PAGE_SIZE] = v_pages[j][...].astype(BF16)
    pad = jnp.zeros((tail - DEC_SEQ, kn_ref.shape[1]), F32)
    k_scr[past:past + tail] = jnp.concatenate([kn_ref[...], pad], axis=0).astype(BF16)
    v_scr[past:past + tail] = jnp.concatenate([vn_ref[...], pad], axis=0).astype(BF16)

    q = q_ref[...]
    qt = jnp.concatenate([q] * (LANES // DEC_SEQ), axis=0)
    rid = lax.broadcasted_iota(jnp.int32, qt.shape, 0)
    blk_id = lax.broadcasted_iota(jnp.int32, qt.shape, 1) // HEAD_DIM
    want = ((rid // DEC_SEQ) % N_HEADS) * 2 + rid // (N_HEADS * DEC_SEQ)
    qm = jnp.where((blk_id == want) & (rid < n_col), qt, 0.0).astype(BF16)

    s_scr[...] = lax.dot_general(k_scr[...], qm, (((1,), (1,)), ((), ())),
                                 preferred_element_type=F32)
    st = s_scr[past:past + tail]
    t_new = lax.broadcasted_iota(jnp.int32, st.shape, 0)
    q_idx = lax.broadcasted_iota(jnp.int32, st.shape, 1) % DEC_SEQ
    s_scr[past:past + tail] = jnp.where(t_new <= q_idx, st, NEG_BIG)

    s = s_scr[...]
    m = jnp.max(s, axis=0, keepdims=True)
    p = jnp.exp(s - m)
    l_row = jnp.sum(p, axis=0, keepdims=True)
    p_scr[...] = p.astype(BF16)
    o_all = lax.dot_general(p_scr[...], v_scr[...], (((0,), (0,)), ((), ())),
                            preferred_element_type=F32)
    eye = (lax.broadcasted_iota(jnp.int32, (LANES, LANES), 0)
           == lax.broadcasted_iota(jnp.int32, (LANES, LANES), 1))
    l_col = jnp.sum(jnp.where(eye, jnp.broadcast_to(l_row, (LANES, LANES)), 0.0),
                    axis=1, keepdims=True)
    o_all = o_all / l_col
    half = N_HEADS * DEC_SEQ
    od = o_all[0:half] - lam_ref[0, 0] * o_all[half:2 * half]
    for h in range(N_HEADS):
        o_h = od[h * DEC_SEQ:(h + 1) * DEC_SEQ, h * V_DIM:(h + 1) * V_DIM]
        o_ref[:, h * V_DIM:(h + 1) * V_DIM] = _subln(o_h, g_ref, post_scale)


def _attn_sample(page_table, lam, q, k_new, v_new, g_subln, k_pool, v_pool, layer, post_scale):
    n_seq, n_pages = page_table.shape
    width = q.shape[1]
    past = n_pages * PAGE_SIZE
    rows = past + 2 * DEC_SEQ
    tok = pl.BlockSpec((DEC_SEQ, width), lambda n, pt: (n, 0))

    def page_spec(j):
        return pl.BlockSpec((None, None, PAGE_SIZE, width),
                            lambda n, pt, j=j: (layer, pt[n, j], 0, 0))

    pages = [page_spec(j) for j in range(n_pages)]
    grid_spec = pltpu.PrefetchScalarGridSpec(
        num_scalar_prefetch=1,
        grid=(n_seq,),
        in_specs=[pl.BlockSpec(memory_space=pltpu.SMEM), tok, tok, tok,
                  pl.BlockSpec((1, V_DIM), lambda n, pt: (0, 0))] + pages + pages,
        out_specs=tok,
        scratch_shapes=[pltpu.VMEM((rows, width), BF16), pltpu.VMEM((rows, width), BF16),
                        pltpu.VMEM((rows, LANES), F32), pltpu.VMEM((rows, LANES), BF16)],
    )
    return pl.pallas_call(
        functools.partial(_attn_sample_kernel, n_pages=n_pages, post_scale=post_scale),
        out_shape=jax.ShapeDtypeStruct((n_seq * DEC_SEQ, width), F32),
        grid_spec=grid_spec,
        compiler_params=_cparams("arbitrary"),
        name="attn_sample",
    )(page_table, lam, q, k_new, v_new, g_subln, *([k_pool] * n_pages), *([v_pool] * n_pages))


def _merge_kernel(x_ref, ys_ref, ya_ref, gate_ref, wbs_ref, wba_ref, wo_ref, g_ref, o_ref):
    d = x_ref.shape[1]
    ys = jnp.concatenate([ys_ref[j] for j in range(ys_ref.shape[0])], axis=-1).astype(BF16)
    ya = ya_ref[...].astype(BF16)
    gate = gate_ref[...]
    merged = (jax.nn.sigmoid(gate[:, :d]) * jnp.dot(ys, wbs_ref[...], preferred_element_type=F32)
              + jax.nn.sigmoid(gate[:, d:]) * jnp.dot(ya, wba_ref[...], preferred_element_type=F32))
    out = jnp.dot(merged.astype(BF16), wo_ref[...], preferred_element_type=F32)
    o_ref[...] = x_ref[...] + _rms(out, g_ref[...], RMS_EPS)


def _merge(x, y_s5, y_att, gate, wbs, wba, wo, g_post):
    t, d = x.shape
    tm = _largest_tile(t, 512)
    n_slab = y_s5.shape[0]

    def rows(width):
        return pl.BlockSpec((tm, width), lambda i: (i, 0))

    return pl.pallas_call(
        _merge_kernel,
        out_shape=jax.ShapeDtypeStruct((t, d), F32),
        grid=(t // tm,),
        in_specs=[rows(d), pl.BlockSpec((n_slab, tm, LANES), lambda i: (0, i, 0)),
                  rows(y_att.shape[1]), rows(gate.shape[1]),
                  _resident(wbs.shape), _resident(wba.shape), _resident(wo.shape),
                  pl.BlockSpec((1, d), lambda i: (0, 0))],
        out_specs=rows(d),
        compiler_params=_cparams("arbitrary"),
        name="merge",
    )(x, y_s5, y_att, gate, wbs, wba, wo, g_post)


def _ffn_weights(w_gate, w_up, w_down):
    depth, d, f = w_gate.shape
    nc = f // FF_CHUNK
    g = w_gate.reshape(depth, d, nc, FF_CHUNK)
    u = w_up.reshape(depth, d, nc, FF_CHUNK)
    wgu = jnp.concatenate([g, u], axis=-1).transpose(0, 2, 1, 3).astype(BF16)
    wd = w_down.reshape(depth, nc, FF_CHUNK, d).astype(BF16)
    return wgu, wd


def _block_diag(m):
    eye = jnp.eye(GROUPS_PER_SLAB, dtype=m.dtype)
    depth, n_slab, g, r, c = m.shape
    return jnp.einsum("ljgrc,gh->ljgrhc", m, eye).reshape(depth, n_slab, g * r, g * c)


def _s5_params(a_re, a_im, log_dt, b_re, b_im, c_re, c_im, d, w_glu, tc):
    depth, groups, n_p = a_re.shape
    n_slab = groups // GROUPS_PER_SLAB
    seg = tc // SUBLANES
    dt = jnp.exp(log_dt)[:, :, None]
    mag = jnp.exp(dt * a_re)
    abar_re, abar_im = mag * jnp.cos(dt * a_im), mag * jnp.sin(dt * a_im)
    den = a_re * a_re + a_im * a_im
    inv_re, inv_im = a_re / den, -a_im / den
    fac_re = (abar_re - 1.0) * inv_re - abar_im * inv_im
    fac_im = (abar_re - 1.0) * inv_im + abar_im * inv_re
    fb_re = fac_re[..., None] * b_re - fac_im[..., None] * b_im
    fb_im = fac_re[..., None] * b_im + fac_im[..., None] * b_re

    def slab(x):
        return x.reshape((depth, n_slab, GROUPS_PER_SLAB) + x.shape[2:])

    bbd = jnp.concatenate([_block_diag(slab(fb_re).swapaxes(-1, -2)),
                           _block_diag(slab(fb_im).swapaxes(-1, -2))], axis=-1).astype(BF16)
    cbd = jnp.concatenate([_block_diag(slab(c_re).swapaxes(-1, -2)),
                           _block_diag(slab(-c_im).swapaxes(-1, -2))], axis=-2).astype(BF16)

    def lanes(x):
        return x.reshape(depth, n_slab, SLAB_STATES)

    pw_re, pw_im = [abar_re], [abar_im]
    for _ in range(seg - 1):
        r, i = pw_re[-1], pw_im[-1]
        pw_re.append(r * abar_re - i * abar_im)
        pw_im.append(r * abar_im + i * abar_re)

    def rows8(x):
        return jnp.broadcast_to(x[..., None, :], x.shape[:-1] + (SUBLANES, x.shape[-1]))

    ab = rows8(jnp.stack([lanes(abar_re), lanes(abar_im)], axis=2))
    aps = rows8(jnp.stack([lanes(pw_re[-1]), lanes(pw_im[-1])], axis=2))
    pw = rows8(jnp.stack([jnp.stack([lanes(x) for x in pw_re], axis=2),
                          jnp.stack([lanes(x) for x in pw_im], axis=2)], axis=2))
    return dict(bbd=bbd, cbd=cbd, ab=ab, aps=aps, pw=pw,
                d=d[:, None, :], wglu=w_glu.astype(BF16))


def kernel(x_prompt, x_sample, cache_k, cache_v, state_s5_re, state_s5_im, page_table, g_ffn1_pre, g_ffn1_post, w_ffn1_gate, w_ffn1_up, w_ffn1_down, g_mix_pre, g_mix_post, w_in, s5_a_re, s5_a_im, s5_log_dt, s5_b_re, s5_b_im, s5_c_re, s5_c_im, s5_d, w_glu, lambda_q1, lambda_k1, lambda_q2, lambda_k2, g_subln, w_branch_s5, w_branch_att, w_out, g_ffn2_pre, g_ffn2_post, w_ffn2_gate, w_ffn2_up, w_ffn2_down):
    n_batch, seq, d_model = x_prompt.shape
    n_seq = x_sample.shape[0]
    depth = w_in.shape[0]
    n_prompt = n_batch * seq
    groups, n_p = s5_a_re.shape[1:]
    d_s5 = groups * S5_GROUP
    d_qk = N_HEADS * 2 * HEAD_DIM
    d_att = N_HEADS * V_DIM
    n_state = groups * n_p

    x = jnp.concatenate([x_prompt.reshape(n_prompt, d_model),
                         x_sample.reshape(n_seq * DEC_SEQ, d_model)], axis=0)

    tc = _largest_tile(seq, 256)
    wgu1, wd1 = _ffn_weights(w_ffn1_gate, w_ffn1_up, w_ffn1_down)
    wgu2, wd2 = _ffn_weights(w_ffn2_gate, w_ffn2_up, w_ffn2_down)
    w_in_b = w_in.astype(BF16)
    wbs_b, wba_b, wo_b = w_branch_s5.astype(BF16), w_branch_att.astype(BF16), w_out.astype(BF16)
    s5p = _s5_params(s5_a_re, s5_a_im, s5_log_dt, s5_b_re, s5_b_im, s5_c_re, s5_c_im,
                     s5_d, w_glu, tc)
    s5p["tc"] = tc
    lam_dyn = (jnp.exp(jnp.sum(lambda_q1 * lambda_k1, axis=-1))
               - jnp.exp(jnp.sum(lambda_q2 * lambda_k2, axis=-1)))
    k_pool = cache_k.reshape(cache_k.shape[:3] + (d_qk,))
    v_pool = cache_v.reshape(cache_v.shape[:3] + (d_att,))
    h0_re = state_s5_re.reshape(depth, n_seq, n_state)
    h0_im = state_s5_im.reshape(depth, n_seq, n_state)

    def vec(g, l):
        return g[l][None, :]

    outs = [[] for _ in range(8)]
    for l in range(depth):
        lambda_init = 0.8 - 0.6 * math.exp(-0.3 * l)
        lam = (lam_dyn[l] + lambda_init).reshape(1, 1)
        p_l = {k: (v if k == "tc" else v[l]) for k, v in s5p.items()}

        x = _ffn(x, vec(g_ffn1_pre, l), vec(g_ffn1_post, l), wgu1[l], wd1[l])
        u, q, k_f, k_b, v_f, v_b, gate = _inproj(x, vec(g_mix_pre, l), w_in_b[l], d_s5, d_qk, d_att)

        y_p, sp_re, sp_im = _s5_prompt(u, n_batch, seq, p_l)
        y_s, ss_re, ss_im = _s5_sample(u[:, n_prompt:], h0_re[l], h0_im[l], p_l)
        g_sub = vec(g_subln, l)
        o_p = _attn_prompt(lam, q, k_b, v_b, g_sub, n_batch, seq, 1.0 - lambda_init)
        o_s = _attn_sample(page_table, lam, q[n_prompt:].astype(F32), k_f[n_prompt:], v_f[n_prompt:],
                           g_sub, k_pool, v_pool, l, 1.0 - lambda_init)
        x = _merge(x, jnp.concatenate([y_p, y_s], axis=1), jnp.concatenate([o_p, o_s], axis=0),
                   gate, wbs_b[l], wba_b[l], wo_b[l], vec(g_mix_post, l))
        x = _ffn(x, vec(g_ffn2_pre, l), vec(g_ffn2_post, l), wgu2[l], wd2[l])

        for lst, val in zip(outs, (k_f[:n_prompt], v_f[:n_prompt], sp_re, sp_im,
                                   k_f[n_prompt:], v_f[n_prompt:], ss_re, ss_im)):
            lst.append(val)

    k_p, v_p, sp_re, sp_im, k_s, v_s, ss_re, ss_im = [jnp.stack(o) for o in outs]
    return (x[:n_prompt].reshape(n_batch, seq, d_model),
            x[n_prompt:].reshape(n_seq, DEC_SEQ, d_model),
            k_p.reshape(depth, n_batch, seq, N_HEADS, 2, HEAD_DIM),
            v_p.reshape(depth, n_batch, seq, N_HEADS, V_DIM),
            sp_re.reshape(depth, n_batch, groups, n_p),
            sp_im.reshape(depth, n_batch, groups, n_p),
            k_s.reshape(depth, n_seq, DEC_SEQ, N_HEADS, 2, HEAD_DIM),
            v_s.reshape(depth, n_seq, DEC_SEQ, N_HEADS, V_DIM),
            ss_re.reshape(depth, n_seq, groups, n_p),
            ss_im.reshape(depth, n_seq, groups, n_p))
```

```python
import functools
import math

import jax
import jax.numpy as jnp
from jax import lax
from jax.experimental import pallas as pl
from jax.experimental.pallas import tpu as pltpu

F32 = jnp.float32
BF16 = jnp.bfloat16

PAGE_SIZE = 128
DEC_SEQ = 8
S5_GROUP = 16
S5_STATE = 64
N_HEADS = 4
HEAD_DIM = 64
V_DIM = 2 * HEAD_DIM
RMS_EPS = 1e-6
SUBLN_EPS = 1e-5
NEG_BIG = -1e30

LANES = 128
SUBLANES = 8
GROUPS_PER_SLAB = LANES // S5_GROUP
SLAB_STATES = GROUPS_PER_SLAB * S5_STATE
FF_CHUNK = 256
VMEM_LIMIT = 56 * 1024 * 1024


def _largest_tile(n, pref):
    t = pref
    while n % t:
        t //= 2
    return t


def _rms(x, g, eps):
    return x * lax.rsqrt(jnp.mean(x * x, axis=-1, keepdims=True) + eps) * g


def _cparams(*sem):
    return pltpu.CompilerParams(dimension_semantics=sem, vmem_limit_bytes=VMEM_LIMIT)


def _resident(shape):
    nd = len(shape)
    return pl.BlockSpec(shape, lambda *_: (0,) * nd, pipeline_mode=pl.Buffered(1))


def _ffn_kernel(x_ref, gpre_ref, gpost_ref, wgu_ref, wd_ref, o_ref, h_scr, acc_scr):
    x = x_ref[...]
    h_scr[...] = _rms(x, gpre_ref[...], RMS_EPS).astype(BF16)
    n_chunks = wgu_ref.shape[0]

    def chunk(c):
        gu = jnp.dot(h_scr[...], wgu_ref[c], preferred_element_type=F32)
        g = gu[:, :FF_CHUNK]
        a = (g * jax.nn.sigmoid(g) * gu[:, FF_CHUNK:]).astype(BF16)
        return jnp.dot(a, wd_ref[c], preferred_element_type=F32)

    acc_scr[...] = chunk(0)

    def body(c, carry):
        acc_scr[...] += chunk(c)
        return carry

    lax.fori_loop(1, n_chunks, body, 0)
    o_ref[...] = x + 0.5 * _rms(acc_scr[...], gpost_ref[...], RMS_EPS)


def _ffn(x, gpre, gpost, wgu, wd):
    t, d = x.shape
    tm = _largest_tile(t, 512)
    row = pl.BlockSpec((tm, d), lambda i: (i, 0))
    vec = pl.BlockSpec((1, d), lambda i: (0, 0))
    return pl.pallas_call(
        _ffn_kernel,
        out_shape=jax.ShapeDtypeStruct((t, d), F32),
        grid=(t // tm,),
        in_specs=[row, vec, vec, _resident(wgu.shape), _resident(wd.shape)],
        out_specs=row,
        scratch_shapes=[pltpu.VMEM((tm, d), BF16), pltpu.VMEM((tm, d), F32)],
        compiler_params=_cparams("arbitrary"),
        name="ffn",
    )(x, gpre, gpost, wgu, wd)


def _inproj_kernel(x_ref, g_ref, w_ref, u_ref, q_ref, kf_ref, kb_ref, vf_ref, vb_ref, gate_ref,
                   *, d_s5, d_qk, d_att):
    h = _rms(x_ref[...], g_ref[...], RMS_EPS).astype(BF16)

    def proj(lo, width):
        return jnp.dot(h, w_ref[:, lo:lo + width], preferred_element_type=F32)

    u = proj(0, d_s5)
    for j in range(d_s5 // LANES):
        u_ref[j] = u[:, j * LANES:(j + 1) * LANES]
    q_ref[...] = (proj(d_s5, d_qk) * (HEAD_DIM ** -0.5)).astype(BF16)
    k = proj(d_s5 + d_qk, d_qk)
    kf_ref[...] = k
    kb_ref[...] = k.astype(BF16)
    v = proj(d_s5 + 2 * d_qk, d_att)
    vf_ref[...] = v
    vb_ref[...] = v.astype(BF16)
    lo = d_s5 + 2 * d_qk + d_att
    gate_ref[...] = proj(lo, w_ref.shape[1] - lo)


def _inproj(x, g, w, d_s5, d_qk, d_att):
    t, d = x.shape
    tm = _largest_tile(t, 512)
    d_gate = w.shape[1] - d_s5 - 2 * d_qk - d_att

    def rows(width):
        return pl.BlockSpec((tm, width), lambda i: (i, 0))

    n_slab = d_s5 // LANES
    return pl.pallas_call(
        functools.partial(_inproj_kernel, d_s5=d_s5, d_qk=d_qk, d_att=d_att),
        out_shape=(jax.ShapeDtypeStruct((n_slab, t, LANES), F32),
                   jax.ShapeDtypeStruct((t, d_qk), BF16),
                   jax.ShapeDtypeStruct((t, d_qk), F32),
                   jax.ShapeDtypeStruct((t, d_qk), BF16),
                   jax.ShapeDtypeStruct((t, d_att), F32),
                   jax.ShapeDtypeStruct((t, d_att), BF16),
                   jax.ShapeDtypeStruct((t, d_gate), F32)),
        grid=(t // tm,),
        in_specs=[rows(d), pl.BlockSpec((1, d), lambda i: (0, 0)), _resident(w.shape)],
        out_specs=(pl.BlockSpec((n_slab, tm, LANES), lambda i: (0, i, 0)),
                   rows(d_qk), rows(d_qk), rows(d_qk), rows(d_att), rows(d_att), rows(d_gate)),
        compiler_params=_cparams("arbitrary"),
        name="in_proj",
    )(x, g, w)


def _s5_readout(st_scr, up_scr, cbd_ref, d_ref, wglu_ref, n_slab):
    ys = []
    for j in range(n_slab):
        hj = st_scr[:, 2 * SLAB_STATES * j:2 * SLAB_STATES * (j + 1)].astype(BF16)
        ys.append(jnp.dot(hj, cbd_ref[j], preferred_element_type=F32))
    y = jnp.concatenate(ys, axis=-1) + d_ref[...] * up_scr[...]
    z = jax.nn.gelu(y).astype(BF16)
    zg = jnp.dot(z, wglu_ref[...], preferred_element_type=F32)
    half = zg.shape[1] // 2
    return zg[:, :half] * jax.nn.sigmoid(zg[:, half:])


def _s5_input(st_scr, up_scr, bbd_ref, n_slab):
    for j in range(n_slab):
        uj = up_scr[:, j * LANES:(j + 1) * LANES].astype(BF16)
        st_scr[:, 2 * SLAB_STATES * j:2 * SLAB_STATES * (j + 1)] = jnp.dot(
            uj, bbd_ref[j], preferred_element_type=F32)


def _s5_prompt_kernel(u_ref, bbd_ref, cbd_ref, ab_ref, aps_ref, pw_ref, d_ref, wglu_ref,
                      y_ref, sre_ref, sim_ref, up_scr, st_scr, carry_scr, *, tc):
    c = pl.program_id(1)
    n_slab = u_ref.shape[0]
    seg = tc // SUBLANES
    w = SLAB_STATES

    @pl.when(c == 0)
    def _():
        carry_scr[...] = jnp.zeros_like(carry_scr)

    for j in range(n_slab):
        for k in range(seg):
            up_scr[k * SUBLANES:(k + 1) * SUBLANES, j * LANES:(j + 1) * LANES] = (
                u_ref[j, pl.ds(k, SUBLANES, stride=seg), :])
    _s5_input(st_scr, up_scr, bbd_ref, n_slab)

    row_id = lax.broadcasted_iota(jnp.int32, (SUBLANES, w), 0)
    for j in range(n_slab):
        re = slice(2 * w * j, 2 * w * j + w)
        im = slice(2 * w * j + w, 2 * w * (j + 1))
        a_re = ab_ref[j, 0]
        a_im = ab_ref[j, 1]

        def scan_step(k, h, re=re, im=im, a_re=a_re, a_im=a_im):
            h_re, h_im = h
            r = pl.ds(pl.multiple_of(k * SUBLANES, SUBLANES), SUBLANES)
            n_re = a_re * h_re - a_im * h_im + st_scr[r, re]
            n_im = a_re * h_im + a_im * h_re + st_scr[r, im]
            st_scr[r, re] = n_re
            st_scr[r, im] = n_im
            return n_re, n_im

        zero = jnp.zeros((SUBLANES, w), F32)
        e_re, e_im = lax.fori_loop(0, seg, scan_step, (zero, zero), unroll=2)

        s_re = aps_ref[j, 0][0:1]
        s_im = aps_ref[j, 1][0:1]
        cur_re = carry_scr[0:1, re]
        cur_im = carry_scr[0:1, im]
        c_re = jnp.broadcast_to(cur_re, (SUBLANES, w))
        c_im = jnp.broadcast_to(cur_im, (SUBLANES, w))
        for i in range(1, SUBLANES + 1):
            nxt_re = s_re * cur_re - s_im * cur_im + e_re[i - 1:i]
            nxt_im = s_re * cur_im + s_im * cur_re + e_im[i - 1:i]
            cur_re, cur_im = nxt_re, nxt_im
            if i < SUBLANES:
                c_re = jnp.where(row_id == i, cur_re, c_re)
                c_im = jnp.where(row_id == i, cur_im, c_im)
        carry_scr[0:1, re] = cur_re
        carry_scr[0:1, im] = cur_im

        def fix_step(k, carry, re=re, im=im, c_re=c_re, c_im=c_im, j=j):
            r = pl.ds(pl.multiple_of(k * SUBLANES, SUBLANES), SUBLANES)
            p_re = pw_ref[j, 0, k]
            p_im = pw_ref[j, 1, k]
            st_scr[r, re] += p_re * c_re - p_im * c_im
            st_scr[r, im] += p_re * c_im + p_im * c_re
            return carry

        lax.fori_loop(0, seg, fix_step, 0, unroll=2)

    up_scr[...] = _s5_readout(st_scr, up_scr, cbd_ref, d_ref, wglu_ref, n_slab)
    for j in range(n_slab):
        for k in range(seg):
            y_ref[j, pl.ds(k, SUBLANES, stride=seg), :] = (
                up_scr[k * SUBLANES:(k + 1) * SUBLANES, j * LANES:(j + 1) * LANES])

    @pl.when(c == pl.num_programs(1) - 1)
    def _():
        for j in range(n_slab):
            sre_ref[0, :, j * w:(j + 1) * w] = carry_scr[0:1, 2 * w * j:2 * w * j + w]
            sim_ref[0, :, j * w:(j + 1) * w] = carry_scr[0:1, 2 * w * j + w:2 * w * (j + 1)]


def _s5_prompt(u_slab, n_batch, seq, p):
    n_slab = u_slab.shape[0]
    d_s5 = n_slab * LANES
    tc = p["tc"]
    n_chunks = seq // tc
    n_state = n_slab * SLAB_STATES
    u_spec = pl.BlockSpec((n_slab, tc, LANES), lambda b, c: (0, b * n_chunks + c, 0))
    st_spec = pl.BlockSpec((1, 1, n_state), lambda b, c: (b, 0, 0))
    return pl.pallas_call(
        functools.partial(_s5_prompt_kernel, tc=tc),
        out_shape=(jax.ShapeDtypeStruct((n_slab, n_batch * seq, LANES), F32),
                   jax.ShapeDtypeStruct((n_batch, 1, n_state), F32),
                   jax.ShapeDtypeStruct((n_batch, 1, n_state), F32)),
        grid=(n_batch, n_chunks),
        in_specs=[u_spec, _resident(p["bbd"].shape), _resident(p["cbd"].shape),
                  _resident(p["ab"].shape), _resident(p["aps"].shape), _resident(p["pw"].shape),
                  _resident(p["d"].shape), _resident(p["wglu"].shape)],
        out_specs=(u_spec, st_spec, st_spec),
        scratch_shapes=[pltpu.VMEM((tc, d_s5), F32),
                        pltpu.VMEM((tc, 2 * n_state), F32),
                        pltpu.VMEM((SUBLANES, 2 * n_state), F32)],
        compiler_params=_cparams("arbitrary", "arbitrary"),
        name="s5_prompt",
    )(u_slab, p["bbd"], p["cbd"], p["ab"], p["aps"], p["pw"], p["d"], p["wglu"])


def _s5_sample_kernel(u_ref, h0re_ref, h0im_ref, bbd_ref, cbd_ref, ab_ref, d_ref, wglu_ref,
                      y_ref, sre_ref, sim_ref, up_scr, st_scr):
    n_slab = u_ref.shape[0]
    n_seq = h0re_ref.shape[0]
    w = SLAB_STATES
    for j in range(n_slab):
        for t in range(DEC_SEQ):
            up_scr[t * n_seq:(t + 1) * n_seq, j * LANES:(j + 1) * LANES] = (
                u_ref[j, pl.ds(t, n_seq, stride=DEC_SEQ), :])
    _s5_input(st_scr, up_scr, bbd_ref, n_slab)

    for j in range(n_slab):
        re = slice(2 * w * j, 2 * w * j + w)
        im = slice(2 * w * j + w, 2 * w * (j + 1))
        a_re = ab_ref[j, 0]
        a_im = ab_ref[j, 1]

        def seq_group(g, carry, re=re, im=im, a_re=a_re, a_im=a_im, j=j):
            n0 = pl.multiple_of(g * SUBLANES, SUBLANES)
            h_re = h0re_ref[pl.ds(n0, SUBLANES), j * w:(j + 1) * w]
            h_im = h0im_ref[pl.ds(n0, SUBLANES), j * w:(j + 1) * w]
            for t in range(DEC_SEQ):
                r = pl.ds(pl.multiple_of(t * n_seq + n0, SUBLANES), SUBLANES)
                n_re = a_re * h_re - a_im * h_im + st_scr[r, re]
                n_im = a_re * h_im + a_im * h_re + st_scr[r, im]
                st_scr[r, re] = n_re
                st_scr[r, im] = n_im
                h_re, h_im = n_re, n_im
            sre_ref[pl.ds(n0, SUBLANES), j * w:(j + 1) * w] = h_re
            sim_ref[pl.ds(n0, SUBLANES), j * w:(j + 1) * w] = h_im
            return carry

        lax.fori_loop(0, n_seq // SUBLANES, seq_group, 0)

    up_scr[...] = _s5_readout(st_scr, up_scr, cbd_ref, d_ref, wglu_ref, n_slab)
    for j in range(n_slab):
        for t in range(DEC_SEQ):
            y_ref[j, pl.ds(t, n_seq, stride=DEC_SEQ), :] = (
                up_scr[t * n_seq:(t + 1) * n_seq, j * LANES:(j + 1) * LANES])


def _s5_sample(u_slab, h0_re, h0_im, p):
    n_slab, rows, _ = u_slab.shape
    n_seq, n_state = h0_re.shape
    args = (u_slab, h0_re, h0_im, p["bbd"], p["cbd"], p["ab"], p["d"], p["wglu"])
    return pl.pallas_call(
        _s5_sample_kernel,
        out_shape=(jax.ShapeDtypeStruct(u_slab.shape, F32),
                   jax.ShapeDtypeStruct((n_seq, n_state), F32),
                   jax.ShapeDtypeStruct((n_seq, n_state), F32)),
        grid=(1,),
        in_specs=[_resident(a.shape) for a in args],
        out_specs=(_resident(u_slab.shape), _resident(h0_re.shape), _resident(h0_re.shape)),
        scratch_shapes=[pltpu.VMEM((rows, n_slab * LANES), F32),
                        pltpu.VMEM((rows, 2 * n_state), F32)],
        compiler_params=_cparams("arbitrary"),
        name="s5_sample",
    )(*args)


def _subln(o, g_ref, post_scale):
    return _rms(o, g_ref[...], SUBLN_EPS) * post_scale


def _attn_prompt_kernel(lam_ref, q_ref, k_ref, v_ref, g_ref, o_ref, q2_scr, m_scr, l_scr, acc_scr,
                        *, blk, chunk, post_scale):
    qi = pl.program_id(2)
    q = q_ref[...]
    lane = lax.broadcasted_iota(jnp.int32, q.shape, 1)
    zero = jnp.zeros_like(q)
    q2_scr[0:blk] = jnp.where(lane < HEAD_DIM, q, zero)
    q2_scr[blk:2 * blk] = jnp.where(lane >= HEAD_DIM, q, zero)
    m_scr[...] = jnp.full_like(m_scr, NEG_BIG)
    l_scr[...] = jnp.zeros_like(l_scr)
    acc_scr[...] = jnp.zeros_like(acc_scr)

    n_rep = blk // LANES

    def block(kj, masked):
        r = pl.ds(pl.multiple_of(kj * blk, blk), blk)
        for c in range(2 * blk // chunk):
            rows = slice(c * chunk, (c + 1) * chunk)
            s = lax.dot_general(q2_scr[rows], k_ref[r, :], (((1,), (1,)), ((), ())),
                                preferred_element_type=F32)
            if masked:
                qpos = (c * chunk) % blk + lax.broadcasted_iota(jnp.int32, s.shape, 0)
                kpos = lax.broadcasted_iota(jnp.int32, s.shape, 1)
                s = jnp.where(kpos <= qpos, s, NEG_BIG)
            m_old = m_scr[rows]
            m_new = jnp.maximum(m_old, jnp.max(s, axis=-1, keepdims=True))
            alpha = jnp.exp(m_old - m_new)
            p = jnp.exp(s - jnp.concatenate([m_new] * n_rep, axis=-1))
            l_scr[rows] = alpha * l_scr[rows] + jnp.sum(p, axis=-1, keepdims=True)
            acc_scr[rows] = alpha * acc_scr[rows] + jnp.dot(p.astype(BF16), v_ref[r, :],
                                                            preferred_element_type=F32)
            m_scr[rows] = m_new

    def full_block(kj, carry):
        block(kj, False)
        return carry

    lax.fori_loop(0, qi, full_block, 0)
    block(qi, True)

    o1 = acc_scr[0:blk] / l_scr[0:blk]
    o2 = acc_scr[blk:2 * blk] / l_scr[blk:2 * blk]
    o_ref[...] = _subln(o1 - lam_ref[0, 0] * o2, g_ref, post_scale)


def _attn_prompt(lam, q, k, v, g_subln, n_batch, seq, post_scale):
    blk = _largest_tile(seq, 512)
    chunk = min(blk, 256)
    nq = seq // blk
    q_spec = pl.BlockSpec((blk, V_DIM), lambda b, h, i: (b * nq + i, h))
    kv_spec = pl.BlockSpec((seq, V_DIM), lambda b, h, i: (b, h))
    return pl.pallas_call(
        functools.partial(_attn_prompt_kernel, blk=blk, chunk=chunk, post_scale=post_scale),
        out_shape=jax.ShapeDtypeStruct((n_batch * seq, N_HEADS * V_DIM), F32),
        grid=(n_batch, N_HEADS, nq),
        in_specs=[pl.BlockSpec(memory_space=pltpu.SMEM), q_spec, kv_spec, kv_spec,
                  pl.BlockSpec((1, V_DIM), lambda b, h, i: (0, 0))],
        out_specs=q_spec,
        scratch_shapes=[pltpu.VMEM((2 * blk, V_DIM), BF16),
                        pltpu.VMEM((2 * blk, LANES), F32),
                        pltpu.VMEM((2 * blk, LANES), F32),
                        pltpu.VMEM((2 * blk, V_DIM), F32)],
        compiler_params=_cparams("arbitrary", "arbitrary", "arbitrary"),
        name="attn_prompt",
    )(lam, q, k, v, g_subln)


def _attn_sample_kernel(pt_ref, lam_ref, q_ref, kn_ref, vn_ref, g_ref, *rest, n_pages, post_scale):
    kt_pages = rest[:n_pages]
    v_pages = rest[n_pages:2 * n_pages]
    o_ref = rest[2 * n_pages]
    kt_scr, v_scr = rest[2 * n_pages + 1:]
    width = q_ref.shape[1]
    n_row = N_HEADS * 2 * DEC_SEQ

    for j in range(n_pages):
        cols = slice(j * PAGE_SIZE, (j + 1) * PAGE_SIZE)
        kt_scr[:, cols] = kt_pages[j][...].astype(BF16)
        for h in range(N_HEADS):
            v_scr[h, cols, :] = v_pages[j][pl.ds(h, PAGE_SIZE, stride=N_HEADS), :].astype(BF16)
    pad = jnp.zeros((PAGE_SIZE - DEC_SEQ, width), F32)
    k_new = jnp.concatenate([kn_ref[...], pad], axis=0).astype(BF16)
    v_new = jnp.concatenate([vn_ref[...], pad], axis=0).astype(BF16)

    qt = jnp.concatenate([q_ref[...]] * (n_row // DEC_SEQ), axis=0)
    rid = lax.broadcasted_iota(jnp.int32, qt.shape, 0)
    blk_id = lax.broadcasted_iota(jnp.int32, qt.shape, 1) // HEAD_DIM
    qm = jnp.where(blk_id == rid // DEC_SEQ, qt, 0.0).astype(BF16)

    s = jnp.dot(qm, kt_scr[...], preferred_element_type=F32)
    s_new = lax.dot_general(qm, k_new, (((1,), (1,)), ((), ())), preferred_element_type=F32)
    t_new = lax.broadcasted_iota(jnp.int32, s_new.shape, 1)
    q_idx = lax.broadcasted_iota(jnp.int32, s_new.shape, 0) % DEC_SEQ
    s_new = jnp.where(t_new <= q_idx, s_new, NEG_BIG)

    m = jnp.maximum(jnp.max(s, axis=-1, keepdims=True), jnp.max(s_new, axis=-1, keepdims=True))
    p = jnp.exp(s - m)
    p_new = jnp.exp(s_new - m)
    l = jnp.sum(p, axis=-1, keepdims=True) + jnp.sum(p_new, axis=-1, keepdims=True)
    pb = p.astype(BF16)
    pb_new = p_new.astype(BF16)
    lam = lam_ref[0, 0]
    for h in range(N_HEADS):
        rows = slice(2 * DEC_SEQ * h, 2 * DEC_SEQ * (h + 1))
        o_h = (jnp.dot(pb[rows], v_scr[h], preferred_element_type=F32)
               + jnp.dot(pb_new[rows], v_new[:, h * V_DIM:(h + 1) * V_DIM],
                         preferred_element_type=F32)) / l[rows]
        o_ref[:, h * V_DIM:(h + 1) * V_DIM] = _subln(
            o_h[0:DEC_SEQ] - lam * o_h[DEC_SEQ:2 * DEC_SEQ], g_ref, post_scale)


def _attn_sample(page_table, lam, q, k_new, v_new, g_subln, kt_pool, v_pool, layer, post_scale):
    n_seq, n_pages = page_table.shape
    width = q.shape[1]
    past = n_pages * PAGE_SIZE
    tok = pl.BlockSpec((DEC_SEQ, width), lambda n, pt: (n, 0))

    def page_spec(shape, j):
        return pl.BlockSpec((None, None) + shape, lambda n, pt, j=j: (layer, pt[n, j], 0, 0))

    kt_specs = [page_spec(kt_pool.shape[2:], j) for j in range(n_pages)]
    v_specs = [page_spec(v_pool.shape[2:], j) for j in range(n_pages)]
    grid_spec = pltpu.PrefetchScalarGridSpec(
        num_scalar_prefetch=1,
        grid=(n_seq,),
        in_specs=[pl.BlockSpec(memory_space=pltpu.SMEM), tok, tok, tok,
                  pl.BlockSpec((1, V_DIM), lambda n, pt: (0, 0))] + kt_specs + v_specs,
        out_specs=tok,
        scratch_shapes=[pltpu.VMEM((width, past), BF16),
                        pltpu.VMEM((N_HEADS, past, V_DIM), BF16)],
    )
    return pl.pallas_call(
        functools.partial(_attn_sample_kernel, n_pages=n_pages, post_scale=post_scale),
        out_shape=jax.ShapeDtypeStruct((n_seq * DEC_SEQ, width), F32),
        grid_spec=grid_spec,
        compiler_params=_cparams("arbitrary"),
        name="attn_sample",
    )(page_table, lam, q, k_new, v_new, g_subln, *([kt_pool] * n_pages), *([v_pool] * n_pages))


def _merge_kernel(x_ref, ys_ref, ya_ref, gate_ref, wbs_ref, wba_ref, wo_ref, g_ref, o_ref):
    d = x_ref.shape[1]
    ys = jnp.concatenate([ys_ref[j] for j in range(ys_ref.shape[0])], axis=-1).astype(BF16)
    ya = ya_ref[...].astype(BF16)
    gate = gate_ref[...]
    merged = (jax.nn.sigmoid(gate[:, :d]) * jnp.dot(ys, wbs_ref[...], preferred_element_type=F32)
              + jax.nn.sigmoid(gate[:, d:]) * jnp.dot(ya, wba_ref[...], preferred_element_type=F32))
    out = jnp.dot(merged.astype(BF16), wo_ref[...], preferred_element_type=F32)
    o_ref[...] = x_ref[...] + _rms(out, g_ref[...], RMS_EPS)


def _merge(x, y_s5, y_att, gate, wbs, wba, wo, g_post):
    t, d = x.shape
    tm = _largest_tile(t, 512)
    n_slab = y_s5.shape[0]

    def rows(width):
        return pl.BlockSpec((tm, width), lambda i: (i, 0))

    return pl.pallas_call(
        _merge_kernel,
        out_shape=jax.ShapeDtypeStruct((t, d), F32),
        grid=(t // tm,),
        in_specs=[rows(d), pl.BlockSpec((n_slab, tm, LANES), lambda i: (0, i, 0)),
                  rows(y_att.shape[1]), rows(gate.shape[1]),
                  _resident(wbs.shape), _resident(wba.shape), _resident(wo.shape),
                  pl.BlockSpec((1, d), lambda i: (0, 0))],
        out_specs=rows(d),
        compiler_params=_cparams("arbitrary"),
        name="merge",
    )(x, y_s5, y_att, gate, wbs, wba, wo, g_post)


def _ffn_weights(w_gate, w_up, w_down):
    depth, d, f = w_gate.shape
    nc = f // FF_CHUNK
    g = w_gate.reshape(depth, d, nc, FF_CHUNK)
    u = w_up.reshape(depth, d, nc, FF_CHUNK)
    wgu = jnp.concatenate([g, u], axis=-1).transpose(0, 2, 1, 3).astype(BF16)
    wd = w_down.reshape(depth, nc, FF_CHUNK, d).astype(BF16)
    return wgu, wd


def _block_diag(m):
    eye = jnp.eye(GROUPS_PER_SLAB, dtype=m.dtype)
    depth, n_slab, g, r, c = m.shape
    return jnp.einsum("ljgrc,gh->ljgrhc", m, eye).reshape(depth, n_slab, g * r, g * c)


def _s5_params(a_re, a_im, log_dt, b_re, b_im, c_re, c_im, d, w_glu, tc):
    depth, groups, n_p = a_re.shape
    n_slab = groups // GROUPS_PER_SLAB
    seg = tc // SUBLANES
    dt = jnp.exp(log_dt)[:, :, None]
    mag = jnp.exp(dt * a_re)
    abar_re, abar_im = mag * jnp.cos(dt * a_im), mag * jnp.sin(dt * a_im)
    den = a_re * a_re + a_im * a_im
    inv_re, inv_im = a_re / den, -a_im / den
    fac_re = (abar_re - 1.0) * inv_re - abar_im * inv_im
    fac_im = (abar_re - 1.0) * inv_im + abar_im * inv_re
    fb_re = fac_re[..., None] * b_re - fac_im[..., None] * b_im
    fb_im = fac_re[..., None] * b_im + fac_im[..., None] * b_re

    def slab(x):
        return x.reshape((depth, n_slab, GROUPS_PER_SLAB) + x.shape[2:])

    bbd = jnp.concatenate([_block_diag(slab(fb_re).swapaxes(-1, -2)),
                           _block_diag(slab(fb_im).swapaxes(-1, -2))], axis=-1).astype(BF16)
    cbd = jnp.concatenate([_block_diag(slab(c_re).swapaxes(-1, -2)),
                           _block_diag(slab(-c_im).swapaxes(-1, -2))], axis=-2).astype(BF16)

    def lanes(x):
        return x.reshape(depth, n_slab, SLAB_STATES)

    pw_re, pw_im = [abar_re], [abar_im]
    for _ in range(seg - 1):
        r, i = pw_re[-1], pw_im[-1]
        pw_re.append(r * abar_re - i * abar_im)
        pw_im.append(r * abar_im + i * abar_re)

    def rows8(x):
        return jnp.broadcast_to(x[..., None, :], x.shape[:-1] + (SUBLANES, x.shape[-1]))

    ab = rows8(jnp.stack([lanes(abar_re), lanes(abar_im)], axis=2))
    aps = rows8(jnp.stack([lanes(pw_re[-1]), lanes(pw_im[-1])], axis=2))
    pw = rows8(jnp.stack([jnp.stack([lanes(x) for x in pw_re], axis=2),
                          jnp.stack([lanes(x) for x in pw_im], axis=2)], axis=2))
    return dict(bbd=bbd, cbd=cbd, ab=ab, aps=aps, pw=pw,
                d=d[:, None, :], wglu=w_glu.astype(BF16))


def kernel(x_prompt, x_sample, cache_k, cache_v, state_s5_re, state_s5_im, page_table, g_ffn1_pre, g_ffn1_post, w_ffn1_gate, w_ffn1_up, w_ffn1_down, g_mix_pre, g_mix_post, w_in, s5_a_re, s5_a_im, s5_log_dt, s5_b_re, s5_b_im, s5_c_re, s5_c_im, s5_d, w_glu, lambda_q1, lambda_k1, lambda_q2, lambda_k2, g_subln, w_branch_s5, w_branch_att, w_out, g_ffn2_pre, g_ffn2_post, w_ffn2_gate, w_ffn2_up, w_ffn2_down):
    n_batch, seq, d_model = x_prompt.shape
    n_seq = x_sample.shape[0]
    depth = w_in.shape[0]
    n_prompt = n_batch * seq
    groups, n_p = s5_a_re.shape[1:]
    d_s5 = groups * S5_GROUP
    d_qk = N_HEADS * 2 * HEAD_DIM
    d_att = N_HEADS * V_DIM
    n_state = groups * n_p

    x = jnp.concatenate([x_prompt.reshape(n_prompt, d_model),
                         x_sample.reshape(n_seq * DEC_SEQ, d_model)], axis=0)

    tc = _largest_tile(seq, 256)
    wgu1, wd1 = _ffn_weights(w_ffn1_gate, w_ffn1_up, w_ffn1_down)
    wgu2, wd2 = _ffn_weights(w_ffn2_gate, w_ffn2_up, w_ffn2_down)
    w_in_b = w_in.astype(BF16)
    wbs_b, wba_b, wo_b = w_branch_s5.astype(BF16), w_branch_att.astype(BF16), w_out.astype(BF16)
    s5p = _s5_params(s5_a_re, s5_a_im, s5_log_dt, s5_b_re, s5_b_im, s5_c_re, s5_c_im,
                     s5_d, w_glu, tc)
    s5p["tc"] = tc
    lam_dyn = (jnp.exp(jnp.sum(lambda_q1 * lambda_k1, axis=-1))
               - jnp.exp(jnp.sum(lambda_q2 * lambda_k2, axis=-1)))
    n_phys = cache_k.shape[1]
    kt_pool = cache_k.transpose(0, 1, 3, 4, 5, 2).reshape(depth, n_phys, d_qk, PAGE_SIZE)
    v_pool = cache_v.reshape(depth, n_phys, PAGE_SIZE * N_HEADS, V_DIM)
    h0_re = state_s5_re.reshape(depth, n_seq, n_state)
    h0_im = state_s5_im.reshape(depth, n_seq, n_state)

    def vec(g, l):
        return g[l][None, :]

    outs = [[] for _ in range(8)]
    for l in range(depth):
        lambda_init = 0.8 - 0.6 * math.exp(-0.3 * l)
        lam = (lam_dyn[l] + lambda_init).reshape(1, 1)
        p_l = {k: (v if k == "tc" else v[l]) for k, v in s5p.items()}

        x = _ffn(x, vec(g_ffn1_pre, l), vec(g_ffn1_post, l), wgu1[l], wd1[l])
        u, q, k_f, k_b, v_f, v_b, gate = _inproj(x, vec(g_mix_pre, l), w_in_b[l], d_s5, d_qk, d_att)

        y_p, sp_re, sp_im = _s5_prompt(u, n_batch, seq, p_l)
        y_s, ss_re, ss_im = _s5_sample(u[:, n_prompt:], h0_re[l], h0_im[l], p_l)
        g_sub = vec(g_subln, l)
        o_p = _attn_prompt(lam, q, k_b, v_b, g_sub, n_batch, seq, 1.0 - lambda_init)
        o_s = _attn_sample(page_table, lam, q[n_prompt:].astype(F32), k_f[n_prompt:], v_f[n_prompt:],
                           g_sub, kt_pool, v_pool, l, 1.0 - lambda_init)
        x = _merge(x, jnp.concatenate([y_p, y_s], axis=1), jnp.concatenate([o_p, o_s], axis=0),
                   gate, wbs_b[l], wba_b[l], wo_b[l], vec(g_mix_post, l))
        x = _ffn(x, vec(g_ffn2_pre, l), vec(g_ffn2_post, l), wgu2[l], wd2[l])

        for lst, val in zip(outs, (k_f[:n_prompt], v_f[:n_prompt], sp_re, sp_im,
                                   k_f[n_prompt:], v_f[n_prompt:], ss_re, ss_im)):
            lst.append(val)

    k_p, v_p, sp_re, sp_im, k_s, v_s, ss_re, ss_im = [jnp.stack(o) for o in outs]
    return (x[:n_prompt].reshape(n_batch, seq, d_model),
            x[n_prompt:].reshape(n_seq, DEC_SEQ, d_model),
            k_p.reshape(depth, n_batch, seq, N_HEADS, 2, HEAD_DIM),
            v_p.reshape(depth, n_batch, seq, N_HEADS, V_DIM),
            sp_re.reshape(depth, n_batch, groups, n_p),
            sp_im.reshape(depth, n_batch, groups, n_p),
            k_s.reshape(depth, n_seq, DEC_SEQ, N_HEADS, 2, HEAD_DIM),
            v_s.reshape(depth, n_seq, DEC_SEQ, N_HEADS, V_DIM),
            ss_re.reshape(depth, n_seq, groups, n_p),
            ss_im.reshape(depth, n_seq, groups, n_p))
```

```python
import functools
import math

import jax
import jax.numpy as jnp
from jax import lax
from jax.experimental import pallas as pl
from jax.experimental.pallas import tpu as pltpu

F32 = jnp.float32
BF16 = jnp.bfloat16

PAGE_SIZE = 128
DEC_SEQ = 8
S5_GROUP = 16
S5_STATE = 64
N_HEADS = 4
HEAD_DIM = 64
V_DIM = 2 * HEAD_DIM
RMS_EPS = 1e-6
SUBLN_EPS = 1e-5
NEG_BIG = -1e30
Q_SCALE = HEAD_DIM ** -0.5 * math.log2(math.e)

LANES = 128
SUBLANES = 8
GROUPS_PER_SLAB = LANES // S5_GROUP
SLAB_STATES = GROUPS_PER_SLAB * S5_STATE
FF_CHUNK = 256
VMEM_LIMIT = 56 * 1024 * 1024


def _largest_tile(n, pref):
    t = pref
    while n % t:
        t //= 2
    return t


def _rms(x, g, eps):
    return x * lax.rsqrt(jnp.mean(x * x, axis=-1, keepdims=True) + eps) * g


def _cparams(*sem):
    return pltpu.CompilerParams(dimension_semantics=sem, vmem_limit_bytes=VMEM_LIMIT)


def _resident(shape):
    nd = len(shape)
    return pl.BlockSpec(shape, lambda *_: (0,) * nd, pipeline_mode=pl.Buffered(1))


def _ffn_kernel(x_ref, gpre_ref, gpost_ref, wgu_ref, wd_ref, o_ref, h_scr, acc_scr):
    x = x_ref[...]
    h_scr[...] = _rms(x, gpre_ref[...], RMS_EPS).astype(BF16)
    n_chunks = wgu_ref.shape[0]

    def chunk(c):
        gu = jnp.dot(h_scr[...], wgu_ref[c], preferred_element_type=F32)
        g = gu[:, :FF_CHUNK]
        a = (g * jax.nn.sigmoid(g) * gu[:, FF_CHUNK:]).astype(BF16)
        return jnp.dot(a, wd_ref[c], preferred_element_type=F32)

    acc_scr[...] = chunk(0)

    def body(c, carry):
        acc_scr[...] += chunk(c)
        return carry

    lax.fori_loop(1, n_chunks, body, 0, unroll=2)
    o_ref[...] = x + 0.5 * _rms(acc_scr[...], gpost_ref[...], RMS_EPS)


def _ffn(x, gpre, gpost, wgu, wd):
    t, d = x.shape
    tm = _largest_tile(t, 1024)
    row = pl.BlockSpec((tm, d), lambda i: (i, 0))
    vec = pl.BlockSpec((1, d), lambda i: (0, 0))
    return pl.pallas_call(
        _ffn_kernel,
        out_shape=jax.ShapeDtypeStruct((t, d), F32),
        grid=(t // tm,),
        in_specs=[row, vec, vec, _resident(wgu.shape), _resident(wd.shape)],
        out_specs=row,
        scratch_shapes=[pltpu.VMEM((tm, d), BF16), pltpu.VMEM((tm, d), F32)],
        compiler_params=_cparams("arbitrary"),
        name="ffn",
    )(x, gpre, gpost, wgu, wd)


def _inproj_kernel(x_ref, g_ref, w_ref, u_ref, qf_ref, qb_ref, kf_ref, kb_ref, vf_ref, vb_ref,
                   gate_ref, *, d_s5, d_qk, d_att):
    h = _rms(x_ref[...], g_ref[...], RMS_EPS).astype(BF16)

    def proj(lo, width):
        return jnp.dot(h, w_ref[:, lo:lo + width], preferred_element_type=F32)

    u = proj(0, d_s5)
    for j in range(d_s5 // LANES):
        u_ref[j] = u[:, j * LANES:(j + 1) * LANES]
    q = proj(d_s5, d_qk) * Q_SCALE
    qf_ref[...] = q
    qb_ref[...] = q.astype(BF16)
    k = proj(d_s5 + d_qk, d_qk)
    kf_ref[...] = k
    kb_ref[...] = k.astype(BF16)
    v = proj(d_s5 + 2 * d_qk, d_att)
    vf_ref[...] = v
    vb_ref[...] = v.astype(BF16)
    lo = d_s5 + 2 * d_qk + d_att
    gate_ref[...] = proj(lo, w_ref.shape[1] - lo)


def _inproj(x, g, w, d_s5, d_qk, d_att):
    t, d = x.shape
    tm = _largest_tile(t, 512)
    d_gate = w.shape[1] - d_s5 - 2 * d_qk - d_att

    def rows(width):
        return pl.BlockSpec((tm, width), lambda i: (i, 0))

    n_slab = d_s5 // LANES
    return pl.pallas_call(
        functools.partial(_inproj_kernel, d_s5=d_s5, d_qk=d_qk, d_att=d_att),
        out_shape=(jax.ShapeDtypeStruct((n_slab, t, LANES), F32),
                   jax.ShapeDtypeStruct((t, d_qk), F32),
                   jax.ShapeDtypeStruct((t, d_qk), BF16),
                   jax.ShapeDtypeStruct((t, d_qk), F32),
                   jax.ShapeDtypeStruct((t, d_qk), BF16),
                   jax.ShapeDtypeStruct((t, d_att), F32),
                   jax.ShapeDtypeStruct((t, d_att), BF16),
                   jax.ShapeDtypeStruct((t, d_gate), F32)),
        grid=(t // tm,),
        in_specs=[rows(d), pl.BlockSpec((1, d), lambda i: (0, 0)), _resident(w.shape)],
        out_specs=(pl.BlockSpec((n_slab, tm, LANES), lambda i: (0, i, 0)),
                   rows(d_qk), rows(d_qk), rows(d_qk), rows(d_qk), rows(d_att), rows(d_att),
                   rows(d_gate)),
        compiler_params=_cparams("arbitrary"),
        name="in_proj",
    )(x, g, w)


def _s5_readout(st_scr, up_scr, cbd_ref, d_ref, wglu_ref, n_slab):
    ys = []
    for j in range(n_slab):
        hj = st_scr[:, 2 * SLAB_STATES * j:2 * SLAB_STATES * (j + 1)].astype(BF16)
        ys.append(jnp.dot(hj, cbd_ref[j], preferred_element_type=F32))
    y = jnp.concatenate(ys, axis=-1) + d_ref[...] * up_scr[...]
    z = jax.nn.gelu(y).astype(BF16)
    zg = jnp.dot(z, wglu_ref[...], preferred_element_type=F32)
    half = zg.shape[1] // 2
    return zg[:, :half] * jax.nn.sigmoid(zg[:, half:])


def _s5_input(st_scr, up_scr, bbd_ref, n_slab):
    for j in range(n_slab):
        uj = up_scr[:, j * LANES:(j + 1) * LANES].astype(BF16)
        st_scr[:, 2 * SLAB_STATES * j:2 * SLAB_STATES * (j + 1)] = jnp.dot(
            uj, bbd_ref[j], preferred_element_type=F32)


def _s5_prompt_kernel(u_ref, bbd_ref, cbd_ref, ab_ref, aps_ref, pw_ref, d_ref, wglu_ref,
                      y_ref, sre_ref, sim_ref, up_scr, st_scr, carry_scr, *, tc):
    c = pl.program_id(1)
    n_slab = u_ref.shape[0]
    seg = tc // SUBLANES
    w = SLAB_STATES

    @pl.when(c == 0)
    def _():
        carry_scr[...] = jnp.zeros_like(carry_scr)

    for j in range(n_slab):
        for k in range(seg):
            up_scr[k * SUBLANES:(k + 1) * SUBLANES, j * LANES:(j + 1) * LANES] = (
                u_ref[j, pl.ds(k, SUBLANES, stride=seg), :])
    _s5_input(st_scr, up_scr, bbd_ref, n_slab)

    row_id = lax.broadcasted_iota(jnp.int32, (SUBLANES, w), 0)
    for j in range(n_slab):
        re = slice(2 * w * j, 2 * w * j + w)
        im = slice(2 * w * j + w, 2 * w * (j + 1))
        a_re = ab_ref[j, 0]
        a_im = ab_ref[j, 1]

        def scan_step(k, h, re=re, im=im, a_re=a_re, a_im=a_im):
            h_re, h_im = h
            r = pl.ds(pl.multiple_of(k * SUBLANES, SUBLANES), SUBLANES)
            n_re = a_re * h_re - a_im * h_im + st_scr[r, re]
            n_im = a_re * h_im + a_im * h_re + st_scr[r, im]
            st_scr[r, re] = n_re
            st_scr[r, im] = n_im
            return n_re, n_im

        zero = jnp.zeros((SUBLANES, w), F32)
        e_re, e_im = lax.fori_loop(0, seg, scan_step, (zero, zero), unroll=2)

        s_re = aps_ref[j, 0][0:1]
        s_im = aps_ref[j, 1][0:1]
        cur_re = carry_scr[0:1, re]
        cur_im = carry_scr[0:1, im]
        c_re = jnp.broadcast_to(cur_re, (SUBLANES, w))
        c_im = jnp.broadcast_to(cur_im, (SUBLANES, w))
        for i in range(1, SUBLANES + 1):
            nxt_re = s_re * cur_re - s_im * cur_im + e_re[i - 1:i]
            nxt_im = s_re * cur_im + s_im * cur_re + e_im[i - 1:i]
            cur_re, cur_im = nxt_re, nxt_im
            if i < SUBLANES:
                c_re = jnp.where(row_id == i, cur_re, c_re)
                c_im = jnp.where(row_id == i, cur_im, c_im)
        carry_scr[0:1, re] = cur_re
        carry_scr[0:1, im] = cur_im

        def fix_step(k, carry, re=re, im=im, c_re=c_re, c_im=c_im, j=j):
            r = pl.ds(pl.multiple_of(k * SUBLANES, SUBLANES), SUBLANES)
            p_re = pw_ref[j, 0, k]
            p_im = pw_ref[j, 1, k]
            st_scr[r, re] += p_re * c_re - p_im * c_im
            st_scr[r, im] += p_re * c_im + p_im * c_re
            return carry

        lax.fori_loop(0, seg, fix_step, 0, unroll=2)

    up_scr[...] = _s5_readout(st_scr, up_scr, cbd_ref, d_ref, wglu_ref, n_slab)
    for j in range(n_slab):
        for k in range(seg):
            y_ref[j, pl.ds(k, SUBLANES, stride=seg), :] = (
                up_scr[k * SUBLANES:(k + 1) * SUBLANES, j * LANES:(j + 1) * LANES])

    @pl.when(c == pl.num_programs(1) - 1)
    def _():
        for j in range(n_slab):
            sre_ref[0, :, j * w:(j + 1) * w] = carry_scr[0:1, 2 * w * j:2 * w * j + w]
            sim_ref[0, :, j * w:(j + 1) * w] = carry_scr[0:1, 2 * w * j + w:2 * w * (j + 1)]


def _s5_prompt(u_slab, n_batch, seq, p):
    n_slab = u_slab.shape[0]
    d_s5 = n_slab * LANES
    tc = p["tc"]
    n_chunks = seq // tc
    n_state = n_slab * SLAB_STATES
    u_spec = pl.BlockSpec((n_slab, tc, LANES), lambda b, c: (0, b * n_chunks + c, 0))
    st_spec = pl.BlockSpec((1, 1, n_state), lambda b, c: (b, 0, 0))
    return pl.pallas_call(
        functools.partial(_s5_prompt_kernel, tc=tc),
        out_shape=(jax.ShapeDtypeStruct(u_slab.shape, F32),
                   jax.ShapeDtypeStruct((n_batch, 1, n_state), F32),
                   jax.ShapeDtypeStruct((n_batch, 1, n_state), F32)),
        grid=(n_batch, n_chunks),
        in_specs=[u_spec, _resident(p["bbd"].shape), _resident(p["cbd"].shape),
                  _resident(p["ab"].shape), _resident(p["aps"].shape), _resident(p["pw"].shape),
                  _resident(p["d"].shape), _resident(p["wglu"].shape)],
        out_specs=(u_spec, st_spec, st_spec),
        scratch_shapes=[pltpu.VMEM((tc, d_s5), F32),
                        pltpu.VMEM((tc, 2 * n_state), F32),
                        pltpu.VMEM((SUBLANES, 2 * n_state), F32)],
        compiler_params=_cparams("arbitrary", "arbitrary"),
        name="s5_prompt",
    )(u_slab, p["bbd"], p["cbd"], p["ab"], p["aps"], p["pw"], p["d"], p["wglu"])


def _s5_sample_kernel(u_ref, y_alias_ref, h0re_ref, h0im_ref, bbd_ref, cbd_ref, ab_ref, d_ref,
                      wglu_ref, y_ref, sre_ref, sim_ref, up_scr, st_scr):
    del y_alias_ref
    n_slab = u_ref.shape[0]
    n_seq = h0re_ref.shape[0]
    w = SLAB_STATES
    for j in range(n_slab):
        for t in range(DEC_SEQ):
            up_scr[t * n_seq:(t + 1) * n_seq, j * LANES:(j + 1) * LANES] = (
                u_ref[j, pl.ds(t, n_seq, stride=DEC_SEQ), :])
    _s5_input(st_scr, up_scr, bbd_ref, n_slab)

    for j in range(n_slab):
        re = slice(2 * w * j, 2 * w * j + w)
        im = slice(2 * w * j + w, 2 * w * (j + 1))
        a_re = ab_ref[j, 0]
        a_im = ab_ref[j, 1]

        def seq_group(g, carry, re=re, im=im, a_re=a_re, a_im=a_im, j=j):
            n0 = pl.multiple_of(g * SUBLANES, SUBLANES)
            h_re = h0re_ref[pl.ds(n0, SUBLANES), j * w:(j + 1) * w]
            h_im = h0im_ref[pl.ds(n0, SUBLANES), j * w:(j + 1) * w]
            for t in range(DEC_SEQ):
                r = pl.ds(pl.multiple_of(t * n_seq + n0, SUBLANES), SUBLANES)
                n_re = a_re * h_re - a_im * h_im + st_scr[r, re]
                n_im = a_re * h_im + a_im * h_re + st_scr[r, im]
                st_scr[r, re] = n_re
                st_scr[r, im] = n_im
                h_re, h_im = n_re, n_im
            sre_ref[pl.ds(n0, SUBLANES), j * w:(j + 1) * w] = h_re
            sim_ref[pl.ds(n0, SUBLANES), j * w:(j + 1) * w] = h_im
            return carry

        lax.fori_loop(0, n_seq // SUBLANES, seq_group, 0)

    up_scr[...] = _s5_readout(st_scr, up_scr, cbd_ref, d_ref, wglu_ref, n_slab)
    for j in range(n_slab):
        for t in range(DEC_SEQ):
            y_ref[j, pl.ds(t, n_seq, stride=DEC_SEQ), :] = (
                up_scr[t * n_seq:(t + 1) * n_seq, j * LANES:(j + 1) * LANES])


def _s5_sample(u_slab, y_slab, h0_re, h0_im, p):
    n_slab, t, _ = u_slab.shape
    n_seq, n_state = h0_re.shape
    rows = n_seq * DEC_SEQ
    tail = pl.BlockSpec((n_slab, rows, LANES), lambda i: (0, (t - rows) // rows, 0))
    params = (h0_re, h0_im, p["bbd"], p["cbd"], p["ab"], p["d"], p["wglu"])
    return pl.pallas_call(
        _s5_sample_kernel,
        out_shape=(jax.ShapeDtypeStruct(y_slab.shape, F32),
                   jax.ShapeDtypeStruct((n_seq, n_state), F32),
                   jax.ShapeDtypeStruct((n_seq, n_state), F32)),
        grid=(1,),
        in_specs=[tail, pl.BlockSpec(memory_space=pl.ANY)] + [_resident(a.shape) for a in params],
        out_specs=(tail, _resident(h0_re.shape), _resident(h0_re.shape)),
        scratch_shapes=[pltpu.VMEM((rows, n_slab * LANES), F32),
                        pltpu.VMEM((rows, 2 * n_state), F32)],
        input_output_aliases={1: 0},
        compiler_params=_cparams("arbitrary"),
        name="s5_sample",
    )(u_slab, y_slab, *params)


def _subln(o, g_ref, post_scale):
    return _rms(o, g_ref[...], SUBLN_EPS) * post_scale


def _attn_prompt_kernel(lam_ref, q_ref, k_ref, v_ref, g_ref, o_ref, q2_scr, m_scr, l_scr, acc_scr,
                        *, blk, post_scale):
    qi = pl.program_id(2)
    q = q_ref[...]
    lane = lax.broadcasted_iota(jnp.int32, q.shape, 1)
    zero = jnp.zeros_like(q)
    q2_scr[0:blk] = jnp.where(lane < HEAD_DIM, q, zero)
    q2_scr[blk:2 * blk] = jnp.where(lane >= HEAD_DIM, q, zero)
    m_scr[...] = jnp.full_like(m_scr, NEG_BIG)
    l_scr[...] = jnp.zeros_like(l_scr)
    acc_scr[...] = jnp.zeros_like(acc_scr)

    n_rep = blk // LANES

    def block(kj, masked):
        r = pl.ds(pl.multiple_of(kj * blk, blk), blk)
        s = lax.dot_general(q2_scr[...], k_ref[r, :], (((1,), (1,)), ((), ())),
                            preferred_element_type=F32)
        if masked:
            qpos = lax.broadcasted_iota(jnp.int32, s.shape, 0) % blk
            kpos = lax.broadcasted_iota(jnp.int32, s.shape, 1)
            s = jnp.where(kpos <= qpos, s, NEG_BIG)
        m_old = m_scr[...]
        m_new = jnp.maximum(m_old, jnp.max(s, axis=-1, keepdims=True))
        alpha = jnp.exp2(m_old - m_new)
        p = jnp.exp2(s - jnp.concatenate([m_new] * n_rep, axis=-1))
        l_scr[...] = alpha * l_scr[...] + jnp.sum(p, axis=-1, keepdims=True)
        acc_scr[...] = alpha * acc_scr[...] + jnp.dot(p.astype(BF16), v_ref[r, :],
                                                      preferred_element_type=F32)
        m_scr[...] = m_new

    def full_block(kj, carry):
        block(kj, False)
        return carry

    lax.fori_loop(0, qi, full_block, 0)
    block(qi, True)

    o1 = acc_scr[0:blk] / l_scr[0:blk]
    o2 = acc_scr[blk:2 * blk] / l_scr[blk:2 * blk]
    o_ref[...] = _subln(o1 - lam_ref[0, 0] * o2, g_ref, post_scale)


def _attn_prompt(lam, q, k, v, g_subln, n_batch, seq, post_scale):
    blk = _largest_tile(seq, 512)
    nq = seq // blk
    q_spec = pl.BlockSpec((blk, V_DIM), lambda b, h, i: (b * nq + i, h))
    kv_spec = pl.BlockSpec((seq, V_DIM), lambda b, h, i: (b, h))
    return pl.pallas_call(
        functools.partial(_attn_prompt_kernel, blk=blk, post_scale=post_scale),
        out_shape=jax.ShapeDtypeStruct((q.shape[0], N_HEADS * V_DIM), F32),
        grid=(n_batch, N_HEADS, nq),
        in_specs=[pl.BlockSpec(memory_space=pltpu.SMEM), q_spec, kv_spec, kv_spec,
                  pl.BlockSpec((1, V_DIM), lambda b, h, i: (0, 0))],
        out_specs=q_spec,
        scratch_shapes=[pltpu.VMEM((2 * blk, V_DIM), BF16),
                        pltpu.VMEM((2 * blk, LANES), F32),
                        pltpu.VMEM((2 * blk, LANES), F32),
                        pltpu.VMEM((2 * blk, V_DIM), F32)],
        compiler_params=_cparams("arbitrary", "arbitrary", "arbitrary"),
        name="attn_prompt",
    )(lam, q, k, v, g_subln)


def _attn_sample_kernel(pt_ref, lam_ref, q_ref, kn_ref, vn_ref, g_ref, o_alias_ref, *rest,
                        n_pages, post_scale):
    del pt_ref, o_alias_ref
    kt_pages = rest[:n_pages]
    v_pages = rest[n_pages:2 * n_pages]
    o_ref = rest[2 * n_pages]
    kt_scr, v_scr = rest[2 * n_pages + 1:]
    width = q_ref.shape[1]
    n_row = N_HEADS * 2 * DEC_SEQ

    for j in range(n_pages):
        cols = slice(j * PAGE_SIZE, (j + 1) * PAGE_SIZE)
        kt_scr[:, cols] = kt_pages[j][...].astype(BF16)
        for h in range(N_HEADS):
            v_scr[h, cols, :] = v_pages[j][pl.ds(h, PAGE_SIZE, stride=N_HEADS), :].astype(BF16)
    pad = jnp.zeros((PAGE_SIZE - DEC_SEQ, width), F32)
    k_new = jnp.concatenate([kn_ref[...], pad], axis=0).astype(BF16)
    v_new = jnp.concatenate([vn_ref[...], pad], axis=0).astype(BF16)

    qt = jnp.concatenate([q_ref[...]] * (n_row // DEC_SEQ), axis=0)
    rid = lax.broadcasted_iota(jnp.int32, qt.shape, 0)
    blk_id = lax.broadcasted_iota(jnp.int32, qt.shape, 1) // HEAD_DIM
    qm = jnp.where(blk_id == rid // DEC_SEQ, qt, 0.0).astype(BF16)

    s = jnp.dot(qm, kt_scr[...], preferred_element_type=F32)
    s_new = lax.dot_general(qm, k_new, (((1,), (1,)), ((), ())), preferred_element_type=F32)
    t_new = lax.broadcasted_iota(jnp.int32, s_new.shape, 1)
    q_idx = lax.broadcasted_iota(jnp.int32, s_new.shape, 0) % DEC_SEQ
    s_new = jnp.where(t_new <= q_idx, s_new, NEG_BIG)

    m = jnp.maximum(jnp.max(s, axis=-1, keepdims=True), jnp.max(s_new, axis=-1, keepdims=True))
    p = jnp.exp2(s - m)
    p_new = jnp.exp2(s_new - m)
    l = jnp.sum(p, axis=-1, keepdims=True) + jnp.sum(p_new, axis=-1, keepdims=True)
    pb = p.astype(BF16)
    pb_new = p_new.astype(BF16)
    lam = lam_ref[0, 0]
    for h in range(N_HEADS):
        rows = slice(2 * DEC_SEQ * h, 2 * DEC_SEQ * (h + 1))
        o_h = (jnp.dot(pb[rows], v_scr[h], preferred_element_type=F32)
               + jnp.dot(pb_new[rows], v_new[:, h * V_DIM:(h + 1) * V_DIM],
                         preferred_element_type=F32)) / l[rows]
        o_ref[:, h * V_DIM:(h + 1) * V_DIM] = _subln(
            o_h[0:DEC_SEQ] - lam * o_h[DEC_SEQ:2 * DEC_SEQ], g_ref, post_scale)


def _attn_sample(page_table, lam, q, k, v, o, g_subln, kt_pool, v_pool, layer, post_scale):
    n_seq, n_pages = page_table.shape
    t, width = q.shape
    past = n_pages * PAGE_SIZE
    first = t // DEC_SEQ - n_seq
    tok = pl.BlockSpec((DEC_SEQ, width), lambda n, pt: (first + n, 0))

    def page_spec(shape, j):
        return pl.BlockSpec((None, None) + shape, lambda n, pt, j=j: (layer, pt[n, j], 0, 0))

    kt_specs = [page_spec(kt_pool.shape[2:], j) for j in range(n_pages)]
    v_specs = [page_spec(v_pool.shape[2:], j) for j in range(n_pages)]
    grid_spec = pltpu.PrefetchScalarGridSpec(
        num_scalar_prefetch=1,
        grid=(n_seq,),
        in_specs=[pl.BlockSpec(memory_space=pltpu.SMEM), tok, tok, tok,
                  pl.BlockSpec((1, V_DIM), lambda n, pt: (0, 0)),
                  pl.BlockSpec(memory_space=pl.ANY)] + kt_specs + v_specs,
        out_specs=tok,
        scratch_shapes=[pltpu.VMEM((width, past), BF16),
                        pltpu.VMEM((N_HEADS, past, V_DIM), BF16)],
    )
    return pl.pallas_call(
        functools.partial(_attn_sample_kernel, n_pages=n_pages, post_scale=post_scale),
        out_shape=jax.ShapeDtypeStruct(o.shape, F32),
        grid_spec=grid_spec,
        input_output_aliases={6: 0},
        compiler_params=_cparams("arbitrary"),
        name="attn_sample",
    )(page_table, lam, q, k, v, g_subln, o, *([kt_pool] * n_pages), *([v_pool] * n_pages))


def _merge_kernel(x_ref, ys_ref, ya_ref, gate_ref, wbs_ref, wba_ref, wo_ref, g_ref, o_ref):
    d = x_ref.shape[1]
    ys = jnp.concatenate([ys_ref[j] for j in range(ys_ref.shape[0])], axis=-1).astype(BF16)
    ya = ya_ref[...].astype(BF16)
    gate = gate_ref[...]
    merged = (jax.nn.sigmoid(gate[:, :d]) * jnp.dot(ys, wbs_ref[...], preferred_element_type=F32)
              + jax.nn.sigmoid(gate[:, d:]) * jnp.dot(ya, wba_ref[...], preferred_element_type=F32))
    out = jnp.dot(merged.astype(BF16), wo_ref[...], preferred_element_type=F32)
    o_ref[...] = x_ref[...] + _rms(out, g_ref[...], RMS_EPS)


def _merge(x, y_s5, y_att, gate, wbs, wba, wo, g_post):
    t, d = x.shape
    tm = _largest_tile(t, 512)
    n_slab = y_s5.shape[0]

    def rows(width):
        return pl.BlockSpec((tm, width), lambda i: (i, 0))

    return pl.pallas_call(
        _merge_kernel,
        out_shape=jax.ShapeDtypeStruct((t, d), F32),
        grid=(t // tm,),
        in_specs=[rows(d), pl.BlockSpec((n_slab, tm, LANES), lambda i: (0, i, 0)),
                  rows(y_att.shape[1]), rows(gate.shape[1]),
                  _resident(wbs.shape), _resident(wba.shape), _resident(wo.shape),
                  pl.BlockSpec((1, d), lambda i: (0, 0))],
        out_specs=rows(d),
        compiler_params=_cparams("arbitrary"),
        name="merge",
    )(x, y_s5, y_att, gate, wbs, wba, wo, g_post)


def _ffn_weights(w_gate, w_up, w_down):
    depth, d, f = w_gate.shape
    nc = f // FF_CHUNK
    g = w_gate.reshape(depth, d, nc, FF_CHUNK)
    u = w_up.reshape(depth, d, nc, FF_CHUNK)
    wgu = jnp.concatenate([g, u], axis=-1).transpose(0, 2, 1, 3).astype(BF16)
    wd = w_down.reshape(depth, nc, FF_CHUNK, d).astype(BF16)
    return wgu, wd


def _block_diag(m):
    eye = jnp.eye(GROUPS_PER_SLAB, dtype=m.dtype)
    depth, n_slab, g, r, c = m.shape
    return jnp.einsum("ljgrc,gh->ljgrhc", m, eye).reshape(depth, n_slab, g * r, g * c)


def _s5_params(a_re, a_im, log_dt, b_re, b_im, c_re, c_im, d, w_glu, tc):
    depth, groups, n_p = a_re.shape
    n_slab = groups // GROUPS_PER_SLAB
    seg = tc // SUBLANES
    dt = jnp.exp(log_dt)[:, :, None]
    mag = jnp.exp(dt * a_re)
    abar_re, abar_im = mag * jnp.cos(dt * a_im), mag * jnp.sin(dt * a_im)
    den = a_re * a_re + a_im * a_im
    inv_re, inv_im = a_re / den, -a_im / den
    fac_re = (abar_re - 1.0) * inv_re - abar_im * inv_im
    fac_im = (abar_re - 1.0) * inv_im + abar_im * inv_re
    fb_re = fac_re[..., None] * b_re - fac_im[..., None] * b_im
    fb_im = fac_re[..., None] * b_im + fac_im[..., None] * b_re

    def slab(x):
        return x.reshape((depth, n_slab, GROUPS_PER_SLAB) + x.shape[2:])

    bbd = jnp.concatenate([_block_diag(slab(fb_re).swapaxes(-1, -2)),
                           _block_diag(slab(fb_im).swapaxes(-1, -2))], axis=-1).astype(BF16)
    cbd = jnp.concatenate([_block_diag(slab(c_re).swapaxes(-1, -2)),
                           _block_diag(slab(-c_im).swapaxes(-1, -2))], axis=-2).astype(BF16)

    def lanes(x):
        return x.reshape(depth, n_slab, SLAB_STATES)

    pw_re, pw_im = [abar_re], [abar_im]
    for _ in range(seg - 1):
        r, i = pw_re[-1], pw_im[-1]
        pw_re.append(r * abar_re - i * abar_im)
        pw_im.append(r * abar_im + i * abar_re)

    def rows8(x):
        return jnp.broadcast_to(x[..., None, :], x.shape[:-1] + (SUBLANES, x.shape[-1]))

    ab = rows8(jnp.stack([lanes(abar_re), lanes(abar_im)], axis=2))
    aps = rows8(jnp.stack([lanes(pw_re[-1]), lanes(pw_im[-1])], axis=2))
    pw = rows8(jnp.stack([jnp.stack([lanes(x) for x in pw_re], axis=2),
                          jnp.stack([lanes(x) for x in pw_im], axis=2)], axis=2))
    return dict(bbd=bbd, cbd=cbd, ab=ab, aps=aps, pw=pw,
                d=d[:, None, :], wglu=w_glu.astype(BF16))


def kernel(x_prompt, x_sample, cache_k, cache_v, state_s5_re, state_s5_im, page_table, g_ffn1_pre, g_ffn1_post, w_ffn1_gate, w_ffn1_up, w_ffn1_down, g_mix_pre, g_mix_post, w_in, s5_a_re, s5_a_im, s5_log_dt, s5_b_re, s5_b_im, s5_c_re, s5_c_im, s5_d, w_glu, lambda_q1, lambda_k1, lambda_q2, lambda_k2, g_subln, w_branch_s5, w_branch_att, w_out, g_ffn2_pre, g_ffn2_post, w_ffn2_gate, w_ffn2_up, w_ffn2_down):
    n_batch, seq, d_model = x_prompt.shape
    n_seq = x_sample.shape[0]
    depth = w_in.shape[0]
    n_prompt = n_batch * seq
    groups, n_p = s5_a_re.shape[1:]
    d_s5 = groups * S5_GROUP
    d_qk = N_HEADS * 2 * HEAD_DIM
    d_att = N_HEADS * V_DIM
    n_state = groups * n_p

    x = jnp.concatenate([x_prompt.reshape(n_prompt, d_model),
                         x_sample.reshape(n_seq * DEC_SEQ, d_model)], axis=0)

    tc = _largest_tile(seq, 256)
    wgu1, wd1 = _ffn_weights(w_ffn1_gate, w_ffn1_up, w_ffn1_down)
    wgu2, wd2 = _ffn_weights(w_ffn2_gate, w_ffn2_up, w_ffn2_down)
    w_in_b = w_in.astype(BF16)
    wbs_b, wba_b, wo_b = w_branch_s5.astype(BF16), w_branch_att.astype(BF16), w_out.astype(BF16)
    s5p = _s5_params(s5_a_re, s5_a_im, s5_log_dt, s5_b_re, s5_b_im, s5_c_re, s5_c_im,
                     s5_d, w_glu, tc)
    s5p["tc"] = tc
    lam_dyn = (jnp.exp(jnp.sum(lambda_q1 * lambda_k1, axis=-1))
               - jnp.exp(jnp.sum(lambda_q2 * lambda_k2, axis=-1)))
    n_phys = cache_k.shape[1]
    kt_pool = cache_k.transpose(0, 1, 3, 4, 5, 2).reshape(depth, n_phys, d_qk, PAGE_SIZE)
    v_pool = cache_v.reshape(depth, n_phys, PAGE_SIZE * N_HEADS, V_DIM)
    h0_re = state_s5_re.reshape(depth, n_seq, n_state)
    h0_im = state_s5_im.reshape(depth, n_seq, n_state)

    def vec(g, l):
        return g[l][None, :]

    outs = [[] for _ in range(8)]
    for l in range(depth):
        lambda_init = 0.8 - 0.6 * math.exp(-0.3 * l)
        lam = (lam_dyn[l] + lambda_init).reshape(1, 1)
        p_l = {k: (v if k == "tc" else v[l]) for k, v in s5p.items()}

        x = _ffn(x, vec(g_ffn1_pre, l), vec(g_ffn1_post, l), wgu1[l], wd1[l])
        u, q_f, q_b, k_f, k_b, v_f, v_b, gate = _inproj(x, vec(g_mix_pre, l), w_in_b[l],
                                                        d_s5, d_qk, d_att)

        y, sp_re, sp_im = _s5_prompt(u, n_batch, seq, p_l)
        y, ss_re, ss_im = _s5_sample(u, y, h0_re[l], h0_im[l], p_l)
        g_sub = vec(g_subln, l)
        o = _attn_prompt(lam, q_b, k_b, v_b, g_sub, n_batch, seq, 1.0 - lambda_init)
        o = _attn_sample(page_table, lam, q_f, k_f, v_f, o, g_sub, kt_pool, v_pool, l,
                         1.0 - lambda_init)
        x = _merge(x, y, o, gate, wbs_b[l], wba_b[l], wo_b[l], vec(g_mix_post, l))
        x = _ffn(x, vec(g_ffn2_pre, l), vec(g_ffn2_post, l), wgu2[l], wd2[l])

        for lst, val in zip(outs, (k_f[:n_prompt], v_f[:n_prompt], sp_re, sp_im,
                                   k_f[n_prompt:], v_f[n_prompt:], ss_re, ss_im)):
            lst.append(val)

    k_p, v_p, sp_re, sp_im, k_s, v_s, ss_re, ss_im = [jnp.stack(o) for o in outs]
    return (x[:n_prompt].reshape(n_batch, seq, d_model),
            x[n_prompt:].reshape(n_seq, DEC_SEQ, d_model),
            k_p.reshape(depth, n_batch, seq, N_HEADS, 2, HEAD_DIM),
            v_p.reshape(depth, n_batch, seq, N_HEADS, V_DIM),
            sp_re.reshape(depth, n_batch, groups, n_p),
            sp_im.reshape(depth, n_batch, groups, n_p),
            k_s.reshape(depth, n_seq, DEC_SEQ, N_HEADS, 2, HEAD_DIM),
            v_s.reshape(depth, n_seq, DEC_SEQ, N_HEADS, V_DIM),
            ss_re.reshape(depth, n_seq, groups, n_p),
            ss_im.reshape(depth, n_seq, groups, n_p))
```

```python
import functools
import math

import jax
import jax.numpy as jnp
from jax import lax
from jax.experimental import pallas as pl
from jax.experimental.pallas import tpu as pltpu

F32 = jnp.float32
BF16 = jnp.bfloat16

PAGE_SIZE = 128
DEC_SEQ = 8
S5_GROUP = 16
S5_STATE = 64
N_HEADS = 4
HEAD_DIM = 64
V_DIM = 2 * HEAD_DIM
RMS_EPS = 1e-6
SUBLN_EPS = 1e-5
NEG_BIG = -1e30
Q_SCALE = HEAD_DIM ** -0.5 * math.log2(math.e)

LANES = 128
SUBLANES = 8
GROUPS_PER_SLAB = LANES // S5_GROUP
SLAB_STATES = GROUPS_PER_SLAB * S5_STATE
FF_CHUNK = 256
VMEM_LIMIT = 56 * 1024 * 1024


def _largest_tile(n, pref):
    t = pref
    while n % t:
        t //= 2
    return t


def _rms(x, g, eps):
    return x * lax.rsqrt(jnp.mean(x * x, axis=-1, keepdims=True) + eps) * g


def _cparams(*sem):
    return pltpu.CompilerParams(dimension_semantics=sem, vmem_limit_bytes=VMEM_LIMIT)


def _resident(shape):
    nd = len(shape)
    return pl.BlockSpec(shape, lambda *_: (0,) * nd, pipeline_mode=pl.Buffered(1))


def _ffn_kernel(x_ref, gpre_ref, gpost_ref, wg_ref, wu_ref, wd_ref, o_ref, h_scr, acc_scr):
    x = x_ref[...]
    h_scr[...] = _rms(x, gpre_ref[...], RMS_EPS).astype(BF16)
    for c in range(wg_ref.shape[1] // FF_CHUNK):
        cols = slice(c * FF_CHUNK, (c + 1) * FF_CHUNK)
        g = jnp.dot(h_scr[...], wg_ref[:, cols], preferred_element_type=F32)
        u = jnp.dot(h_scr[...], wu_ref[:, cols], preferred_element_type=F32)
        a = (g * jax.nn.sigmoid(g) * u).astype(BF16)
        part = jnp.dot(a, wd_ref[cols, :], preferred_element_type=F32)
        if c == 0:
            acc_scr[...] = part
        else:
            acc_scr[...] += part
    o_ref[...] = x + 0.5 * _rms(acc_scr[...], gpost_ref[...], RMS_EPS)


def _ffn(x, gpre, gpost, wg, wu, wd):
    t, d = x.shape
    tm = _largest_tile(t, 1024)
    row = pl.BlockSpec((tm, d), lambda i: (i, 0))
    vec = pl.BlockSpec((1, d), lambda i: (0, 0))
    return pl.pallas_call(
        _ffn_kernel,
        out_shape=jax.ShapeDtypeStruct((t, d), F32),
        grid=(t // tm,),
        in_specs=[row, vec, vec, _resident(wg.shape), _resident(wu.shape), _resident(wd.shape)],
        out_specs=row,
        scratch_shapes=[pltpu.VMEM((tm, d), BF16), pltpu.VMEM((tm, d), F32)],
        compiler_params=_cparams("arbitrary"),
        name="ffn",
    )(x, gpre, gpost, wg, wu, wd)


def _inproj_kernel(x_ref, g_ref, w_ref, *rest, d_s5, d_qk, d_att):
    u_ref, qf_ref, qb_ref, kf_ref, kb_ref, vf_ref, vb_ref, gate_ref = rest[-8:]
    h = _rms(x_ref[...], g_ref[...], RMS_EPS).astype(BF16)

    def proj(lo, width):
        return jnp.dot(h, w_ref[:, lo:lo + width], preferred_element_type=F32)

    u = proj(0, d_s5)
    for j in range(d_s5 // LANES):
        u_ref[j] = u[:, j * LANES:(j + 1) * LANES]
    q = proj(d_s5, d_qk) * Q_SCALE
    qf_ref[...] = q
    qb_ref[...] = q.astype(BF16)
    k = proj(d_s5 + d_qk, d_qk)
    kf_ref[...] = k
    kb_ref[...] = k.astype(BF16)
    v = proj(d_s5 + 2 * d_qk, d_att)
    vf_ref[...] = v
    vb_ref[...] = v.astype(BF16)
    lo = d_s5 + 2 * d_qk + d_att
    gate_ref[...] = proj(lo, w_ref.shape[1] - lo)


def _inproj(x, g, w, kv_stacks, layer, depth, d_s5, d_qk, d_att):
    t, d = x.shape
    tm = _largest_tile(t, 512)
    d_gate = w.shape[1] - d_s5 - 2 * d_qk - d_att

    def rows(width):
        return pl.BlockSpec((tm, width), lambda i: (i, 0))

    def stack_rows(width):
        return pl.BlockSpec((None, tm, width), lambda i: (layer, i, 0))

    n_slab = d_s5 // LANES
    any_spec = pl.BlockSpec(memory_space=pl.ANY)
    return pl.pallas_call(
        functools.partial(_inproj_kernel, d_s5=d_s5, d_qk=d_qk, d_att=d_att),
        out_shape=(jax.ShapeDtypeStruct((n_slab, t, LANES), F32),
                   jax.ShapeDtypeStruct((t, d_qk), F32),
                   jax.ShapeDtypeStruct((t, d_qk), BF16),
                   jax.ShapeDtypeStruct((depth, t, d_qk), F32),
                   jax.ShapeDtypeStruct((t, d_qk), BF16),
                   jax.ShapeDtypeStruct((depth, t, d_att), F32),
                   jax.ShapeDtypeStruct((t, d_att), BF16),
                   jax.ShapeDtypeStruct((t, d_gate), F32)),
        grid=(t // tm,),
        in_specs=[rows(d), pl.BlockSpec((1, d), lambda i: (0, 0)), _resident(w.shape)]
        + [any_spec] * len(kv_stacks),
        out_specs=(pl.BlockSpec((n_slab, tm, LANES), lambda i: (0, i, 0)),
                   rows(d_qk), rows(d_qk), stack_rows(d_qk), rows(d_qk), stack_rows(d_att),
                   rows(d_att), rows(d_gate)),
        input_output_aliases={3: 3, 4: 5} if kv_stacks else {},
        compiler_params=_cparams("arbitrary"),
        name="in_proj",
    )(x, g, w, *kv_stacks)


def _s5_readout(st_scr, up_scr, cbd_ref, d_ref, wglu_ref, n_slab):
    ys = []
    for j in range(n_slab):
        hj = st_scr[:, 2 * SLAB_STATES * j:2 * SLAB_STATES * (j + 1)].astype(BF16)
        ys.append(jnp.dot(hj, cbd_ref[j], preferred_element_type=F32))
    y = jnp.concatenate(ys, axis=-1) + d_ref[...] * up_scr[...]
    z = jax.nn.gelu(y).astype(BF16)
    zg = jnp.dot(z, wglu_ref[...], preferred_element_type=F32)
    half = zg.shape[1] // 2
    return zg[:, :half] * jax.nn.sigmoid(zg[:, half:])


def _s5_input(st_scr, up_scr, bbd_ref, n_slab):
    for j in range(n_slab):
        uj = up_scr[:, j * LANES:(j + 1) * LANES].astype(BF16)
        st_scr[:, 2 * SLAB_STATES * j:2 * SLAB_STATES * (j + 1)] = jnp.dot(
            uj, bbd_ref[j], preferred_element_type=F32)


def _s5_prompt_kernel(u_ref, bbd_ref, cbd_ref, ab_ref, aps_ref, pw_ref, d_ref, wglu_ref,
                      y_ref, sre_ref, sim_ref, up_scr, st_scr, carry_scr, *, tc):
    c = pl.program_id(1)
    n_slab = u_ref.shape[0]
    seg = tc // SUBLANES
    w = SLAB_STATES

    @pl.when(c == 0)
    def _():
        carry_scr[...] = jnp.zeros_like(carry_scr)

    for j in range(n_slab):
        for k in range(seg):
            up_scr[k * SUBLANES:(k + 1) * SUBLANES, j * LANES:(j + 1) * LANES] = (
                u_ref[j, pl.ds(k, SUBLANES, stride=seg), :])
    _s5_input(st_scr, up_scr, bbd_ref, n_slab)

    row_id = lax.broadcasted_iota(jnp.int32, (SUBLANES, w), 0)
    for j in range(n_slab):
        re = slice(2 * w * j, 2 * w * j + w)
        im = slice(2 * w * j + w, 2 * w * (j + 1))
        a_re = ab_ref[j, 0]
        a_im = ab_ref[j, 1]

        def scan_step(k, h, re=re, im=im, a_re=a_re, a_im=a_im):
            h_re, h_im = h
            r = pl.ds(pl.multiple_of(k * SUBLANES, SUBLANES), SUBLANES)
            n_re = a_re * h_re - a_im * h_im + st_scr[r, re]
            n_im = a_re * h_im + a_im * h_re + st_scr[r, im]
            st_scr[r, re] = n_re
            st_scr[r, im] = n_im
            return n_re, n_im

        zero = jnp.zeros((SUBLANES, w), F32)
        e_re, e_im = lax.fori_loop(0, seg, scan_step, (zero, zero), unroll=True)

        s_re = aps_ref[j, 0][0:1]
        s_im = aps_ref[j, 1][0:1]
        cur_re = carry_scr[0:1, re]
        cur_im = carry_scr[0:1, im]
        c_re = jnp.broadcast_to(cur_re, (SUBLANES, w))
        c_im = jnp.broadcast_to(cur_im, (SUBLANES, w))
        for i in range(1, SUBLANES + 1):
            nxt_re = s_re * cur_re - s_im * cur_im + e_re[i - 1:i]
            nxt_im = s_re * cur_im + s_im * cur_re + e_im[i - 1:i]
            cur_re, cur_im = nxt_re, nxt_im
            if i < SUBLANES:
                c_re = jnp.where(row_id == i, cur_re, c_re)
                c_im = jnp.where(row_id == i, cur_im, c_im)
        carry_scr[0:1, re] = cur_re
        carry_scr[0:1, im] = cur_im

        def fix_step(k, carry, re=re, im=im, c_re=c_re, c_im=c_im, j=j):
            r = pl.ds(pl.multiple_of(k * SUBLANES, SUBLANES), SUBLANES)
            p_re = pw_ref[j, 0, k]
            p_im = pw_ref[j, 1, k]
            st_scr[r, re] += p_re * c_re - p_im * c_im
            st_scr[r, im] += p_re * c_im + p_im * c_re
            return carry

        lax.fori_loop(0, seg, fix_step, 0, unroll=True)

    up_scr[...] = _s5_readout(st_scr, up_scr, cbd_ref, d_ref, wglu_ref, n_slab)
    for j in range(n_slab):
        for k in range(seg):
            y_ref[j, pl.ds(k, SUBLANES, stride=seg), :] = (
                up_scr[k * SUBLANES:(k + 1) * SUBLANES, j * LANES:(j + 1) * LANES])

    @pl.when(c == pl.num_programs(1) - 1)
    def _():
        for j in range(n_slab):
            sre_ref[0, :, j * w:(j + 1) * w] = carry_scr[0:1, 2 * w * j:2 * w * j + w]
            sim_ref[0, :, j * w:(j + 1) * w] = carry_scr[0:1, 2 * w * j + w:2 * w * (j + 1)]


def _s5_prompt(u_slab, n_batch, seq, p):
    n_slab = u_slab.shape[0]
    d_s5 = n_slab * LANES
    tc = p["tc"]
    n_chunks = seq // tc
    n_state = n_slab * SLAB_STATES
    u_spec = pl.BlockSpec((n_slab, tc, LANES), lambda b, c: (0, b * n_chunks + c, 0))
    st_spec = pl.BlockSpec((1, 1, n_state), lambda b, c: (b, 0, 0))
    return pl.pallas_call(
        functools.partial(_s5_prompt_kernel, tc=tc),
        out_shape=(jax.ShapeDtypeStruct(u_slab.shape, F32),
                   jax.ShapeDtypeStruct((n_batch, 1, n_state), F32),
                   jax.ShapeDtypeStruct((n_batch, 1, n_state), F32)),
        grid=(n_batch, n_chunks),
        in_specs=[u_spec, _resident(p["bbd"].shape), _resident(p["cbd"].shape),
                  _resident(p["ab"].shape), _resident(p["aps"].shape), _resident(p["pw"].shape),
                  _resident(p["d"].shape), _resident(p["wglu"].shape)],
        out_specs=(u_spec, st_spec, st_spec),
        scratch_shapes=[pltpu.VMEM((tc, d_s5), F32),
                        pltpu.VMEM((tc, 2 * n_state), F32),
                        pltpu.VMEM((SUBLANES, 2 * n_state), F32)],
        compiler_params=_cparams("arbitrary", "arbitrary"),
        name="s5_prompt",
    )(u_slab, p["bbd"], p["cbd"], p["ab"], p["aps"], p["pw"], p["d"], p["wglu"])


def _s5_sample_kernel(u_ref, y_alias_ref, h0re_ref, h0im_ref, bbd_ref, cbd_ref, ab_ref, d_ref,
                      wglu_ref, y_ref, sre_ref, sim_ref, up_scr, st_scr):
    del y_alias_ref
    n_slab = u_ref.shape[0]
    n_seq = h0re_ref.shape[0]
    w = SLAB_STATES
    for j in range(n_slab):
        for t in range(DEC_SEQ):
            up_scr[t * n_seq:(t + 1) * n_seq, j * LANES:(j + 1) * LANES] = (
                u_ref[j, pl.ds(t, n_seq, stride=DEC_SEQ), :])
    _s5_input(st_scr, up_scr, bbd_ref, n_slab)

    for j in range(n_slab):
        re = slice(2 * w * j, 2 * w * j + w)
        im = slice(2 * w * j + w, 2 * w * (j + 1))
        a_re = ab_ref[j, 0]
        a_im = ab_ref[j, 1]

        def seq_group(g, carry, re=re, im=im, a_re=a_re, a_im=a_im, j=j):
            n0 = pl.multiple_of(g * SUBLANES, SUBLANES)
            h_re = h0re_ref[pl.ds(n0, SUBLANES), j * w:(j + 1) * w]
            h_im = h0im_ref[pl.ds(n0, SUBLANES), j * w:(j + 1) * w]
            for t in range(DEC_SEQ):
                r = pl.ds(pl.multiple_of(t * n_seq + n0, SUBLANES), SUBLANES)
                n_re = a_re * h_re - a_im * h_im + st_scr[r, re]
                n_im = a_re * h_im + a_im * h_re + st_scr[r, im]
                st_scr[r, re] = n_re
                st_scr[r, im] = n_im
                h_re, h_im = n_re, n_im
            sre_ref[pl.ds(n0, SUBLANES), j * w:(j + 1) * w] = h_re
            sim_ref[pl.ds(n0, SUBLANES), j * w:(j + 1) * w] = h_im
            return carry

        lax.fori_loop(0, n_seq // SUBLANES, seq_group, 0)

    up_scr[...] = _s5_readout(st_scr, up_scr, cbd_ref, d_ref, wglu_ref, n_slab)
    for j in range(n_slab):
        for t in range(DEC_SEQ):
            y_ref[j, pl.ds(t, n_seq, stride=DEC_SEQ), :] = (
                up_scr[t * n_seq:(t + 1) * n_seq, j * LANES:(j + 1) * LANES])


def _s5_sample(u_slab, y_slab, h0_re, h0_im, p):
    n_slab, t, _ = u_slab.shape
    n_seq, n_state = h0_re.shape
    rows = n_seq * DEC_SEQ
    tail = pl.BlockSpec((n_slab, rows, LANES), lambda i: (0, (t - rows) // rows, 0))
    params = (h0_re, h0_im, p["bbd"], p["cbd"], p["ab"], p["d"], p["wglu"])
    return pl.pallas_call(
        _s5_sample_kernel,
        out_shape=(jax.ShapeDtypeStruct(y_slab.shape, F32),
                   jax.ShapeDtypeStruct((n_seq, n_state), F32),
                   jax.ShapeDtypeStruct((n_seq, n_state), F32)),
        grid=(1,),
        in_specs=[tail, pl.BlockSpec(memory_space=pl.ANY)] + [_resident(a.shape) for a in params],
        out_specs=(tail, _resident(h0_re.shape), _resident(h0_re.shape)),
        scratch_shapes=[pltpu.VMEM((rows, n_slab * LANES), F32),
                        pltpu.VMEM((rows, 2 * n_state), F32)],
        input_output_aliases={1: 0},
        compiler_params=_cparams("arbitrary"),
        name="s5_sample",
    )(u_slab, y_slab, *params)


def _subln(o, g_ref, post_scale):
    return _rms(o, g_ref[...], SUBLN_EPS) * post_scale


def _attn_prompt_kernel(lam_ref, q_ref, k_ref, v_ref, g_ref, o_ref, q2_scr, m_scr, l_scr, acc_scr,
                        *, blk, post_scale):
    qi = pl.program_id(2)
    q = q_ref[...]
    lane = lax.broadcasted_iota(jnp.int32, q.shape, 1)
    zero = jnp.zeros_like(q)
    q2_scr[0:blk] = jnp.where(lane < HEAD_DIM, q, zero)
    q2_scr[blk:2 * blk] = jnp.where(lane >= HEAD_DIM, q, zero)
    m_scr[...] = jnp.full_like(m_scr, NEG_BIG)
    l_scr[...] = jnp.zeros_like(l_scr)
    acc_scr[...] = jnp.zeros_like(acc_scr)

    n_rep = blk // LANES

    def block(kj, masked):
        r = pl.ds(pl.multiple_of(kj * blk, blk), blk)
        s = lax.dot_general(q2_scr[...], k_ref[r, :], (((1,), (1,)), ((), ())),
                            preferred_element_type=F32)
        if masked:
            qpos = lax.broadcasted_iota(jnp.int32, s.shape, 0) % blk
            kpos = lax.broadcasted_iota(jnp.int32, s.shape, 1)
            s = jnp.where(kpos <= qpos, s, NEG_BIG)
        m_old = m_scr[...]
        m_new = jnp.maximum(m_old, jnp.max(s, axis=-1, keepdims=True))
        alpha = jnp.exp2(m_old - m_new)
        p = jnp.exp2(s - jnp.concatenate([m_new] * n_rep, axis=-1))
        l_scr[...] = alpha * l_scr[...] + jnp.sum(p, axis=-1, keepdims=True)
        acc_scr[...] = alpha * acc_scr[...] + jnp.dot(p.astype(BF16), v_ref[r, :],
                                                      preferred_element_type=F32)
        m_scr[...] = m_new

    def full_block(kj, carry):
        block(kj, False)
        return carry

    lax.fori_loop(0, qi, full_block, 0)
    block(qi, True)

    o1 = acc_scr[0:blk] / l_scr[0:blk]
    o2 = acc_scr[blk:2 * blk] / l_scr[blk:2 * blk]
    o_ref[...] = _subln(o1 - lam_ref[0, 0] * o2, g_ref, post_scale)


def _attn_prompt(lam, q, k, v, g_subln, n_batch, seq, post_scale):
    blk = _largest_tile(seq, 512)
    nq = seq // blk
    q_spec = pl.BlockSpec((blk, V_DIM), lambda b, h, i: (b * nq + i, h))
    kv_spec = pl.BlockSpec((seq, V_DIM), lambda b, h, i: (b, h))
    return pl.pallas_call(
        functools.partial(_attn_prompt_kernel, blk=blk, post_scale=post_scale),
        out_shape=jax.ShapeDtypeStruct((q.shape[0], N_HEADS * V_DIM), F32),
        grid=(n_batch, N_HEADS, nq),
        in_specs=[pl.BlockSpec(memory_space=pltpu.SMEM), q_spec, kv_spec, kv_spec,
                  pl.BlockSpec((1, V_DIM), lambda b, h, i: (0, 0))],
        out_specs=q_spec,
        scratch_shapes=[pltpu.VMEM((2 * blk, V_DIM), BF16),
                        pltpu.VMEM((2 * blk, LANES), F32),
                        pltpu.VMEM((2 * blk, LANES), F32),
                        pltpu.VMEM((2 * blk, V_DIM), F32)],
        compiler_params=_cparams("arbitrary", "arbitrary", "arbitrary"),
        name="attn_prompt",
    )(lam, q, k, v, g_subln)


def _attn_sample_kernel(pt_ref, lam_ref, q_ref, kn_ref, vn_ref, g_ref, o_alias_ref, *rest,
                        n_pages, post_scale):
    del pt_ref, o_alias_ref
    kt_pages = rest[:n_pages]
    v_pages = rest[n_pages:2 * n_pages]
    o_ref = rest[2 * n_pages]
    kt_scr, v_scr = rest[2 * n_pages + 1:]
    width = q_ref.shape[1]
    n_row = N_HEADS * 2 * DEC_SEQ

    for j in range(n_pages):
        cols = slice(j * PAGE_SIZE, (j + 1) * PAGE_SIZE)
        kt_scr[:, cols] = kt_pages[j][...].astype(BF16)
        for h in range(N_HEADS):
            v_scr[h, cols, :] = v_pages[j][pl.ds(h, PAGE_SIZE, stride=N_HEADS), :].astype(BF16)
    pad = jnp.zeros((PAGE_SIZE - DEC_SEQ, width), F32)
    k_new = jnp.concatenate([kn_ref[...], pad], axis=0).astype(BF16)
    v_new = jnp.concatenate([vn_ref[...], pad], axis=0).astype(BF16)

    qt = jnp.concatenate([q_ref[...]] * (n_row // DEC_SEQ), axis=0)
    rid = lax.broadcasted_iota(jnp.int32, qt.shape, 0)
    blk_id = lax.broadcasted_iota(jnp.int32, qt.shape, 1) // HEAD_DIM
    qm = jnp.where(blk_id == rid // DEC_SEQ, qt, 0.0).astype(BF16)

    s = jnp.dot(qm, kt_scr[...], preferred_element_type=F32)
    s_new = lax.dot_general(qm, k_new, (((1,), (1,)), ((), ())), preferred_element_type=F32)
    t_new = lax.broadcasted_iota(jnp.int32, s_new.shape, 1)
    q_idx = lax.broadcasted_iota(jnp.int32, s_new.shape, 0) % DEC_SEQ
    s_new = jnp.where(t_new <= q_idx, s_new, NEG_BIG)

    m = jnp.maximum(jnp.max(s, axis=-1, keepdims=True), jnp.max(s_new, axis=-1, keepdims=True))
    p = jnp.exp2(s - m)
    p_new = jnp.exp2(s_new - m)
    l = jnp.sum(p, axis=-1, keepdims=True) + jnp.sum(p_new, axis=-1, keepdims=True)
    pb = p.astype(BF16)
    pb_new = p_new.astype(BF16)
    lam = lam_ref[0, 0]
    for h in range(N_HEADS):
        rows = slice(2 * DEC_SEQ * h, 2 * DEC_SEQ * (h + 1))
        o_h = (jnp.dot(pb[rows], v_scr[h], preferred_element_type=F32)
               + jnp.dot(pb_new[rows], v_new[:, h * V_DIM:(h + 1) * V_DIM],
                         preferred_element_type=F32)) / l[rows]
        o_ref[:, h * V_DIM:(h + 1) * V_DIM] = _subln(
            o_h[0:DEC_SEQ] - lam * o_h[DEC_SEQ:2 * DEC_SEQ], g_ref, post_scale)


def _attn_sample(page_table, lam, q, k, v, o, g_subln, kt_pool, v_pool, layer, post_scale):
    n_seq, n_pages = page_table.shape
    t, width = q.shape
    past = n_pages * PAGE_SIZE
    first = t // DEC_SEQ - n_seq
    tok = pl.BlockSpec((DEC_SEQ, width), lambda n, pt: (first + n, 0))
    new_kv = pl.BlockSpec((None, DEC_SEQ, width), lambda n, pt: (layer, first + n, 0))

    def page_spec(shape, j):
        return pl.BlockSpec((None, None) + shape, lambda n, pt, j=j: (layer, pt[n, j], 0, 0))

    kt_specs = [page_spec(kt_pool.shape[2:], j) for j in range(n_pages)]
    v_specs = [page_spec(v_pool.shape[2:], j) for j in range(n_pages)]
    grid_spec = pltpu.PrefetchScalarGridSpec(
        num_scalar_prefetch=1,
        grid=(n_seq,),
        in_specs=[pl.BlockSpec(memory_space=pltpu.SMEM), tok, new_kv, new_kv,
                  pl.BlockSpec((1, V_DIM), lambda n, pt: (0, 0)),
                  pl.BlockSpec(memory_space=pl.ANY)] + kt_specs + v_specs,
        out_specs=tok,
        scratch_shapes=[pltpu.VMEM((width, past), BF16),
                        pltpu.VMEM((N_HEADS, past, V_DIM), BF16)],
    )
    return pl.pallas_call(
        functools.partial(_attn_sample_kernel, n_pages=n_pages, post_scale=post_scale),
        out_shape=jax.ShapeDtypeStruct(o.shape, F32),
        grid_spec=grid_spec,
        input_output_aliases={6: 0},
        compiler_params=_cparams("arbitrary"),
        name="attn_sample",
    )(page_table, lam, q, k, v, g_subln, o, *([kt_pool] * n_pages), *([v_pool] * n_pages))


def _merge_kernel(x_ref, ys_ref, ya_ref, gate_ref, wbs_ref, wba_ref, wo_ref, g_ref, o_ref):
    d = x_ref.shape[1]
    ys = jnp.concatenate([ys_ref[j] for j in range(ys_ref.shape[0])], axis=-1).astype(BF16)
    ya = ya_ref[...].astype(BF16)
    gate = gate_ref[...]
    merged = (jax.nn.sigmoid(gate[:, :d]) * jnp.dot(ys, wbs_ref[...], preferred_element_type=F32)
              + jax.nn.sigmoid(gate[:, d:]) * jnp.dot(ya, wba_ref[...], preferred_element_type=F32))
    out = jnp.dot(merged.astype(BF16), wo_ref[...], preferred_element_type=F32)
    o_ref[...] = x_ref[...] + _rms(out, g_ref[...], RMS_EPS)


def _merge(x, y_s5, y_att, gate, wbs, wba, wo, g_post):
    t, d = x.shape
    tm = _largest_tile(t, 512)
    n_slab = y_s5.shape[0]

    def rows(width):
        return pl.BlockSpec((tm, width), lambda i: (i, 0))

    return pl.pallas_call(
        _merge_kernel,
        out_shape=jax.ShapeDtypeStruct((t, d), F32),
        grid=(t // tm,),
        in_specs=[rows(d), pl.BlockSpec((n_slab, tm, LANES), lambda i: (0, i, 0)),
                  rows(y_att.shape[1]), rows(gate.shape[1]),
                  _resident(wbs.shape), _resident(wba.shape), _resident(wo.shape),
                  pl.BlockSpec((1, d), lambda i: (0, 0))],
        out_specs=rows(d),
        compiler_params=_cparams("arbitrary"),
        name="merge",
    )(x, y_s5, y_att, gate, wbs, wba, wo, g_post)


def _block_diag(m):
    eye = jnp.eye(GROUPS_PER_SLAB, dtype=m.dtype)
    depth, n_slab, g, r, c = m.shape
    return jnp.einsum("ljgrc,gh->ljgrhc", m, eye).reshape(depth, n_slab, g * r, g * c)


def _s5_params(a_re, a_im, log_dt, b_re, b_im, c_re, c_im, d, w_glu, tc):
    depth, groups, n_p = a_re.shape
    n_slab = groups // GROUPS_PER_SLAB
    seg = tc // SUBLANES
    dt = jnp.exp(log_dt)[:, :, None]
    mag = jnp.exp(dt * a_re)
    abar_re, abar_im = mag * jnp.cos(dt * a_im), mag * jnp.sin(dt * a_im)
    den = a_re * a_re + a_im * a_im
    inv_re, inv_im = a_re / den, -a_im / den
    fac_re = (abar_re - 1.0) * inv_re - abar_im * inv_im
    fac_im = (abar_re - 1.0) * inv_im + abar_im * inv_re
    fb_re = fac_re[..., None] * b_re - fac_im[..., None] * b_im
    fb_im = fac_re[..., None] * b_im + fac_im[..., None] * b_re

    def slab(x):
        return x.reshape((depth, n_slab, GROUPS_PER_SLAB) + x.shape[2:])

    bbd = jnp.concatenate([_block_diag(slab(fb_re).swapaxes(-1, -2)),
                           _block_diag(slab(fb_im).swapaxes(-1, -2))], axis=-1).astype(BF16)
    cbd = jnp.concatenate([_block_diag(slab(c_re).swapaxes(-1, -2)),
                           _block_diag(slab(-c_im).swapaxes(-1, -2))], axis=-2).astype(BF16)

    def lanes(x):
        return x.reshape(depth, n_slab, SLAB_STATES)

    k = jnp.arange(1, seg + 1, dtype=F32)[None, :, None, None]
    pw_mag = jnp.exp(k * (dt * a_re)[:, None])
    pw_arg = k * (dt * a_im)[:, None]
    pw_re, pw_im = pw_mag * jnp.cos(pw_arg), pw_mag * jnp.sin(pw_arg)

    def rows8(x):
        return jnp.broadcast_to(x[..., None, :], x.shape[:-1] + (SUBLANES, x.shape[-1]))

    def seg_lanes(x):
        return x.reshape(depth, seg, n_slab, SLAB_STATES).swapaxes(1, 2)

    ab = rows8(jnp.stack([lanes(abar_re), lanes(abar_im)], axis=2))
    aps = rows8(jnp.stack([lanes(pw_re[:, -1]), lanes(pw_im[:, -1])], axis=2))
    pw = rows8(jnp.stack([seg_lanes(pw_re), seg_lanes(pw_im)], axis=2))
    return dict(bbd=bbd, cbd=cbd, ab=ab, aps=aps, pw=pw,
                d=d[:, None, :], wglu=w_glu.astype(BF16))


def kernel(x_prompt, x_sample, cache_k, cache_v, state_s5_re, state_s5_im, page_table, g_ffn1_pre, g_ffn1_post, w_ffn1_gate, w_ffn1_up, w_ffn1_down, g_mix_pre, g_mix_post, w_in, s5_a_re, s5_a_im, s5_log_dt, s5_b_re, s5_b_im, s5_c_re, s5_c_im, s5_d, w_glu, lambda_q1, lambda_k1, lambda_q2, lambda_k2, g_subln, w_branch_s5, w_branch_att, w_out, g_ffn2_pre, g_ffn2_post, w_ffn2_gate, w_ffn2_up, w_ffn2_down):
    n_batch, seq, d_model = x_prompt.shape
    n_seq = x_sample.shape[0]
    depth = w_in.shape[0]
    n_prompt = n_batch * seq
    groups, n_p = s5_a_re.shape[1:]
    d_s5 = groups * S5_GROUP
    d_qk = N_HEADS * 2 * HEAD_DIM
    d_att = N_HEADS * V_DIM
    n_state = groups * n_p

    x = jnp.concatenate([x_prompt.reshape(n_prompt, d_model),
                         x_sample.reshape(n_seq * DEC_SEQ, d_model)], axis=0)

    tc = _largest_tile(seq, 256)
    ffn1 = [w.astype(BF16) for w in (w_ffn1_gate, w_ffn1_up, w_ffn1_down)]
    ffn2 = [w.astype(BF16) for w in (w_ffn2_gate, w_ffn2_up, w_ffn2_down)]
    w_in_b = w_in.astype(BF16)
    wbs_b, wba_b, wo_b = w_branch_s5.astype(BF16), w_branch_att.astype(BF16), w_out.astype(BF16)
    s5p = _s5_params(s5_a_re, s5_a_im, s5_log_dt, s5_b_re, s5_b_im, s5_c_re, s5_c_im,
                     s5_d, w_glu, tc)
    s5p["tc"] = tc
    lam_dyn = (jnp.exp(jnp.sum(lambda_q1 * lambda_k1, axis=-1))
               - jnp.exp(jnp.sum(lambda_q2 * lambda_k2, axis=-1)))
    n_phys = cache_k.shape[1]
    kt_pool = cache_k.transpose(0, 1, 3, 4, 5, 2).reshape(depth, n_phys, d_qk, PAGE_SIZE)
    v_pool = cache_v.reshape(depth, n_phys, PAGE_SIZE * N_HEADS, V_DIM)
    h0_re = state_s5_re.reshape(depth, n_seq, n_state)
    h0_im = state_s5_im.reshape(depth, n_seq, n_state)

    def vec(g, l):
        return g[l][None, :]

    kv_stacks = ()
    s5_states = [[] for _ in range(4)]
    for l in range(depth):
        lambda_init = 0.8 - 0.6 * math.exp(-0.3 * l)
        lam = (lam_dyn[l] + lambda_init).reshape(1, 1)
        p_l = {k: (v if k == "tc" else v[l]) for k, v in s5p.items()}

        x = _ffn(x, vec(g_ffn1_pre, l), vec(g_ffn1_post, l), *(w[l] for w in ffn1))
        u, q_f, q_b, k_stack, k_b, v_stack, v_b, gate = _inproj(
            x, vec(g_mix_pre, l), w_in_b[l], kv_stacks, l, depth, d_s5, d_qk, d_att)
        kv_stacks = (k_stack, v_stack)

        y, sp_re, sp_im = _s5_prompt(u, n_batch, seq, p_l)
        y, ss_re, ss_im = _s5_sample(u, y, h0_re[l], h0_im[l], p_l)
        g_sub = vec(g_subln, l)
        o = _attn_prompt(lam, q_b, k_b, v_b, g_sub, n_batch, seq, 1.0 - lambda_init)
        o = _attn_sample(page_table, lam, q_f, k_stack, v_stack, o, g_sub, kt_pool, v_pool, l,
                         1.0 - lambda_init)
        x = _merge(x, y, o, gate, wbs_b[l], wba_b[l], wo_b[l], vec(g_mix_post, l))
        x = _ffn(x, vec(g_ffn2_pre, l), vec(g_ffn2_post, l), *(w[l] for w in ffn2))

        for lst, val in zip(s5_states, (sp_re, sp_im, ss_re, ss_im)):
            lst.append(val)

    k_stack, v_stack = kv_stacks
    sp_re, sp_im, ss_re, ss_im = [jnp.stack(s) for s in s5_states]
    return (x[:n_prompt].reshape(n_batch, seq, d_model),
            x[n_prompt:].reshape(n_seq, DEC_SEQ, d_model),
            k_stack[:, :n_prompt].reshape(depth, n_batch, seq, N_HEADS, 2, HEAD_DIM),
            v_stack[:, :n_prompt].reshape(depth, n_batch, seq, N_HEADS, V_DIM),
            sp_re.reshape(depth, n_batch, groups, n_p),
            sp_im.reshape(depth, n_batch, groups, n_p),
            k_stack[:, n_prompt:].reshape(depth, n_seq, DEC_SEQ, N_HEADS, 2, HEAD_DIM),
            v_stack[:, n_prompt:].reshape(depth, n_seq, DEC_SEQ, N_HEADS, V_DIM),
            ss_re.reshape(depth, n_seq, groups, n_p),
            ss_im.reshape(depth, n_seq, groups, n_p))
```

```python
import functools
import math

import jax
import jax.numpy as jnp
from jax import lax
from jax.experimental import pallas as pl
from jax.experimental.pallas import tpu as pltpu

F32 = jnp.float32
BF16 = jnp.bfloat16

PAGE_SIZE = 128
DEC_SEQ = 8
S5_GROUP = 16
S5_STATE = 64
N_HEADS = 4
HEAD_DIM = 64
V_DIM = 2 * HEAD_DIM
RMS_EPS = 1e-6
SUBLN_EPS = 1e-5
NEG_BIG = -1e30
Q_SCALE = HEAD_DIM ** -0.5 * math.log2(math.e)

LANES = 128
SUBLANES = 8
GROUPS_PER_SLAB = LANES // S5_GROUP
SLAB_STATES = GROUPS_PER_SLAB * S5_STATE
FF_CHUNK = 256
VMEM_LIMIT = 56 * 1024 * 1024


def _largest_tile(n, pref):
    t = pref
    while n % t:
        t //= 2
    return t


def _rms(x, g, eps):
    return x * lax.rsqrt(jnp.mean(x * x, axis=-1, keepdims=True) + eps) * g


def _cparams(*sem):
    return pltpu.CompilerParams(dimension_semantics=sem, vmem_limit_bytes=VMEM_LIMIT)


def _resident(shape):
    nd = len(shape)
    return pl.BlockSpec(shape, lambda *_: (0,) * nd, pipeline_mode=pl.Buffered(1))


def _ffn_kernel(x_ref, gpre_ref, gpost_ref, wg_ref, wu_ref, wd_ref, o_ref, h_scr, acc_scr):
    x = x_ref[...]
    h_scr[...] = _rms(x, gpre_ref[...], RMS_EPS).astype(BF16)
    for c in range(wg_ref.shape[1] // FF_CHUNK):
        cols = slice(c * FF_CHUNK, (c + 1) * FF_CHUNK)
        g = jnp.dot(h_scr[...], wg_ref[:, cols], preferred_element_type=F32)
        u = jnp.dot(h_scr[...], wu_ref[:, cols], preferred_element_type=F32)
        a = (g * jax.nn.sigmoid(g) * u).astype(BF16)
        part = jnp.dot(a, wd_ref[cols, :], preferred_element_type=F32)
        if c == 0:
            acc_scr[...] = part
        else:
            acc_scr[...] += part
    o_ref[...] = x + 0.5 * _rms(acc_scr[...], gpost_ref[...], RMS_EPS)


def _ffn(x, gpre, gpost, wg, wu, wd):
    t, d = x.shape
    tm = _largest_tile(t, 1024)
    row = pl.BlockSpec((tm, d), lambda i: (i, 0))
    vec = pl.BlockSpec((1, d), lambda i: (0, 0))
    return pl.pallas_call(
        _ffn_kernel,
        out_shape=jax.ShapeDtypeStruct((t, d), F32),
        grid=(t // tm,),
        in_specs=[row, vec, vec, _resident(wg.shape), _resident(wu.shape), _resident(wd.shape)],
        out_specs=row,
        scratch_shapes=[pltpu.VMEM((tm, d), BF16), pltpu.VMEM((tm, d), F32)],
        compiler_params=_cparams("arbitrary"),
        name="ffn",
    )(x, gpre, gpost, wg, wu, wd)


def _inproj_kernel(x_ref, g_ref, w_ref, *rest, d_s5, d_qk, d_att, n_prompt_tiles):
    (u_ref, qf_ref, qb_ref, kb_ref, vb_ref, gate_ref,
     ktp_ref, ks_ref, vp_ref, vs_ref) = rest[-10:]
    i = pl.program_id(0)
    h = _rms(x_ref[...], g_ref[...], RMS_EPS).astype(BF16)

    def proj(lo, width):
        return jnp.dot(h, w_ref[:, lo:lo + width], preferred_element_type=F32)

    u = proj(0, d_s5)
    for j in range(d_s5 // LANES):
        u_ref[j] = u[:, j * LANES:(j + 1) * LANES]
    q = proj(d_s5, d_qk) * Q_SCALE
    qf_ref[...] = q
    qb_ref[...] = q.astype(BF16)
    k = proj(d_s5 + d_qk, d_qk)
    kb_ref[...] = k.astype(BF16)
    v = proj(d_s5 + 2 * d_qk, d_att)
    vb_ref[...] = v.astype(BF16)
    lo = d_s5 + 2 * d_qk + d_att
    gate_ref[...] = proj(lo, w_ref.shape[1] - lo)

    def store_heads(v_ref):
        for hd in range(N_HEADS):
            v_ref[pl.ds(hd, v.shape[0], stride=N_HEADS), :] = v[:, hd * V_DIM:(hd + 1) * V_DIM]

    @pl.when(i < n_prompt_tiles)
    def _():
        ktp_ref[...] = k.T
        store_heads(vp_ref)

    @pl.when(i >= n_prompt_tiles)
    def _():
        ks_ref[...] = k
        store_heads(vs_ref)


def _inproj(x, g, w, stacks, layer, depth, n_batch, seq, d_s5, d_qk, d_att):
    t, d = x.shape
    tm = _largest_tile(math.gcd(seq, t - n_batch * seq), 512)
    np_tiles = n_batch * seq // tm
    per_batch = seq // tm
    rows_s = t - n_batch * seq
    d_gate = w.shape[1] - d_s5 - 2 * d_qk - d_att

    def rows(width):
        return pl.BlockSpec((tm, width), lambda i: (i, 0))

    def p_tile(i):
        return jnp.minimum(i, np_tiles - 1)

    def s_tile(i):
        return jnp.maximum(i - np_tiles, 0)

    n_slab = d_s5 // LANES
    any_spec = pl.BlockSpec(memory_space=pl.ANY)
    n_in = 3
    return pl.pallas_call(
        functools.partial(_inproj_kernel, d_s5=d_s5, d_qk=d_qk, d_att=d_att,
                          n_prompt_tiles=np_tiles),
        out_shape=(jax.ShapeDtypeStruct((n_slab, t, LANES), F32),
                   jax.ShapeDtypeStruct((t, d_qk), F32),
                   jax.ShapeDtypeStruct((t, d_qk), BF16),
                   jax.ShapeDtypeStruct((t, d_qk), BF16),
                   jax.ShapeDtypeStruct((t, d_att), BF16),
                   jax.ShapeDtypeStruct((t, d_gate), F32),
                   jax.ShapeDtypeStruct((depth, n_batch, d_qk, seq), F32),
                   jax.ShapeDtypeStruct((depth, rows_s, d_qk), F32),
                   jax.ShapeDtypeStruct((depth, n_batch * seq * N_HEADS, V_DIM), F32),
                   jax.ShapeDtypeStruct((depth, rows_s * N_HEADS, V_DIM), F32)),
        grid=(t // tm,),
        in_specs=[rows(d), pl.BlockSpec((1, d), lambda i: (0, 0)), _resident(w.shape)]
        + [any_spec] * len(stacks),
        out_specs=(pl.BlockSpec((n_slab, tm, LANES), lambda i: (0, i, 0)),
                   rows(d_qk), rows(d_qk), rows(d_qk), rows(d_att), rows(d_gate),
                   pl.BlockSpec((None, None, d_qk, tm),
                                lambda i: (layer, p_tile(i) // per_batch, 0, p_tile(i) % per_batch)),
                   pl.BlockSpec((None, tm, d_qk), lambda i: (layer, s_tile(i), 0)),
                   pl.BlockSpec((None, tm * N_HEADS, V_DIM), lambda i: (layer, p_tile(i), 0)),
                   pl.BlockSpec((None, tm * N_HEADS, V_DIM), lambda i: (layer, s_tile(i), 0))),
        input_output_aliases={n_in + s: 6 + s for s in range(len(stacks))},
        compiler_params=_cparams("arbitrary"),
        name="in_proj",
    )(x, g, w, *stacks)


def _s5_readout(st_scr, up_scr, cbd_ref, d_ref, wglu_ref, n_slab):
    ys = []
    for j in range(n_slab):
        hj = st_scr[:, 2 * SLAB_STATES * j:2 * SLAB_STATES * (j + 1)].astype(BF16)
        ys.append(jnp.dot(hj, cbd_ref[j], preferred_element_type=F32))
    y = jnp.concatenate(ys, axis=-1) + d_ref[...] * up_scr[...]
    z = jax.nn.gelu(y).astype(BF16)
    zg = jnp.dot(z, wglu_ref[...], preferred_element_type=F32)
    half = zg.shape[1] // 2
    return zg[:, :half] * jax.nn.sigmoid(zg[:, half:])


def _s5_input(st_scr, up_scr, bbd_ref, n_slab):
    for j in range(n_slab):
        uj = up_scr[:, j * LANES:(j + 1) * LANES].astype(BF16)
        st_scr[:, 2 * SLAB_STATES * j:2 * SLAB_STATES * (j + 1)] = jnp.dot(
            uj, bbd_ref[j], preferred_element_type=F32)


def _s5_prompt_kernel(u_ref, bbd_ref, cbd_ref, ab_ref, aps_ref, pw_ref, d_ref, wglu_ref,
                      y_ref, sre_ref, sim_ref, up_scr, st_scr, carry_scr, *, tc):
    c = pl.program_id(1)
    n_slab = u_ref.shape[0]
    seg = tc // SUBLANES
    w = SLAB_STATES

    @pl.when(c == 0)
    def _():
        carry_scr[...] = jnp.zeros_like(carry_scr)

    for j in range(n_slab):
        for k in range(seg):
            up_scr[k * SUBLANES:(k + 1) * SUBLANES, j * LANES:(j + 1) * LANES] = (
                u_ref[j, pl.ds(k, SUBLANES, stride=seg), :])
    _s5_input(st_scr, up_scr, bbd_ref, n_slab)

    row_id = lax.broadcasted_iota(jnp.int32, (SUBLANES, w), 0)
    for j in range(n_slab):
        re = slice(2 * w * j, 2 * w * j + w)
        im = slice(2 * w * j + w, 2 * w * (j + 1))
        a_re = ab_ref[j, 0]
        a_im = ab_ref[j, 1]

        def scan_step(k, h, re=re, im=im, a_re=a_re, a_im=a_im):
            h_re, h_im = h
            r = pl.ds(pl.multiple_of(k * SUBLANES, SUBLANES), SUBLANES)
            n_re = a_re * h_re - a_im * h_im + st_scr[r, re]
            n_im = a_re * h_im + a_im * h_re + st_scr[r, im]
            st_scr[r, re] = n_re
            st_scr[r, im] = n_im
            return n_re, n_im

        zero = jnp.zeros((SUBLANES, w), F32)
        e_re, e_im = lax.fori_loop(0, seg, scan_step, (zero, zero), unroll=True)

        s_re = aps_ref[j, 0][0:1]
        s_im = aps_ref[j, 1][0:1]
        cur_re = carry_scr[0:1, re]
        cur_im = carry_scr[0:1, im]
        c_re = jnp.broadcast_to(cur_re, (SUBLANES, w))
        c_im = jnp.broadcast_to(cur_im, (SUBLANES, w))
        for i in range(1, SUBLANES + 1):
            nxt_re = s_re * cur_re - s_im * cur_im + e_re[i - 1:i]
            nxt_im = s_re * cur_im + s_im * cur_re + e_im[i - 1:i]
            cur_re, cur_im = nxt_re, nxt_im
            if i < SUBLANES:
                c_re = jnp.where(row_id == i, cur_re, c_re)
                c_im = jnp.where(row_id == i, cur_im, c_im)
        carry_scr[0:1, re] = cur_re
        carry_scr[0:1, im] = cur_im

        def fix_step(k, carry, re=re, im=im, c_re=c_re, c_im=c_im, j=j):
            r = pl.ds(pl.multiple_of(k * SUBLANES, SUBLANES), SUBLANES)
            p_re = pw_ref[j, 0, k]
            p_im = pw_ref[j, 1, k]
            st_scr[r, re] += p_re * c_re - p_im * c_im
            st_scr[r, im] += p_re * c_im + p_im * c_re
            return carry

        lax.fori_loop(0, seg, fix_step, 0, unroll=True)

    up_scr[...] = _s5_readout(st_scr, up_scr, cbd_ref, d_ref, wglu_ref, n_slab)
    for j in range(n_slab):
        for k in range(seg):
            y_ref[j, pl.ds(k, SUBLANES, stride=seg), :] = (
                up_scr[k * SUBLANES:(k + 1) * SUBLANES, j * LANES:(j + 1) * LANES])

    @pl.when(c == pl.num_programs(1) - 1)
    def _():
        for j in range(n_slab):
            sre_ref[0, :, j * w:(j + 1) * w] = carry_scr[0:1, 2 * w * j:2 * w * j + w]
            sim_ref[0, :, j * w:(j + 1) * w] = carry_scr[0:1, 2 * w * j + w:2 * w * (j + 1)]


def _s5_prompt(u_slab, n_batch, seq, p):
    n_slab = u_slab.shape[0]
    d_s5 = n_slab * LANES
    tc = p["tc"]
    n_chunks = seq // tc
    n_state = n_slab * SLAB_STATES
    u_spec = pl.BlockSpec((n_slab, tc, LANES), lambda b, c: (0, b * n_chunks + c, 0))
    st_spec = pl.BlockSpec((1, 1, n_state), lambda b, c: (b, 0, 0))
    return pl.pallas_call(
        functools.partial(_s5_prompt_kernel, tc=tc),
        out_shape=(jax.ShapeDtypeStruct(u_slab.shape, F32),
                   jax.ShapeDtypeStruct((n_batch, 1, n_state), F32),
                   jax.ShapeDtypeStruct((n_batch, 1, n_state), F32)),
        grid=(n_batch, n_chunks),
        in_specs=[u_spec, _resident(p["bbd"].shape), _resident(p["cbd"].shape),
                  _resident(p["ab"].shape), _resident(p["aps"].shape), _resident(p["pw"].shape),
                  _resident(p["d"].shape), _resident(p["wglu"].shape)],
        out_specs=(u_spec, st_spec, st_spec),
        scratch_shapes=[pltpu.VMEM((tc, d_s5), F32),
                        pltpu.VMEM((tc, 2 * n_state), F32),
                        pltpu.VMEM((SUBLANES, 2 * n_state), F32)],
        compiler_params=_cparams("arbitrary", "arbitrary"),
        name="s5_prompt",
    )(u_slab, p["bbd"], p["cbd"], p["ab"], p["aps"], p["pw"], p["d"], p["wglu"])


def _s5_sample_kernel(u_ref, y_alias_ref, h0re_ref, h0im_ref, bbd_ref, cbd_ref, ab_ref, d_ref,
                      wglu_ref, y_ref, sre_ref, sim_ref, up_scr, st_scr):
    del y_alias_ref
    n_slab = u_ref.shape[0]
    n_seq = h0re_ref.shape[0]
    w = SLAB_STATES
    for j in range(n_slab):
        for t in range(DEC_SEQ):
            up_scr[t * n_seq:(t + 1) * n_seq, j * LANES:(j + 1) * LANES] = (
                u_ref[j, pl.ds(t, n_seq, stride=DEC_SEQ), :])
    _s5_input(st_scr, up_scr, bbd_ref, n_slab)

    for j in range(n_slab):
        re = slice(2 * w * j, 2 * w * j + w)
        im = slice(2 * w * j + w, 2 * w * (j + 1))
        a_re = ab_ref[j, 0]
        a_im = ab_ref[j, 1]

        def seq_group(g, carry, re=re, im=im, a_re=a_re, a_im=a_im, j=j):
            n0 = pl.multiple_of(g * SUBLANES, SUBLANES)
            h_re = h0re_ref[pl.ds(n0, SUBLANES), j * w:(j + 1) * w]
            h_im = h0im_ref[pl.ds(n0, SUBLANES), j * w:(j + 1) * w]
            for t in range(DEC_SEQ):
                r = pl.ds(pl.multiple_of(t * n_seq + n0, SUBLANES), SUBLANES)
                n_re = a_re * h_re - a_im * h_im + st_scr[r, re]
                n_im = a_re * h_im + a_im * h_re + st_scr[r, im]
                st_scr[r, re] = n_re
                st_scr[r, im] = n_im
                h_re, h_im = n_re, n_im
            sre_ref[pl.ds(n0, SUBLANES), j * w:(j + 1) * w] = h_re
            sim_ref[pl.ds(n0, SUBLANES), j * w:(j + 1) * w] = h_im
            return carry

        lax.fori_loop(0, n_seq // SUBLANES, seq_group, 0)

    up_scr[...] = _s5_readout(st_scr, up_scr, cbd_ref, d_ref, wglu_ref, n_slab)
    for j in range(n_slab):
        for t in range(DEC_SEQ):
            y_ref[j, pl.ds(t, n_seq, stride=DEC_SEQ), :] = (
                up_scr[t * n_seq:(t + 1) * n_seq, j * LANES:(j + 1) * LANES])


def _s5_sample(u_slab, y_slab, h0_re, h0_im, p):
    n_slab, t, _ = u_slab.shape
    n_seq, n_state = h0_re.shape
    rows = n_seq * DEC_SEQ
    tail = pl.BlockSpec((n_slab, rows, LANES), lambda i: (0, (t - rows) // rows, 0))
    params = (h0_re, h0_im, p["bbd"], p["cbd"], p["ab"], p["d"], p["wglu"])
    return pl.pallas_call(
        _s5_sample_kernel,
        out_shape=(jax.ShapeDtypeStruct(y_slab.shape, F32),
                   jax.ShapeDtypeStruct((n_seq, n_state), F32),
                   jax.ShapeDtypeStruct((n_seq, n_state), F32)),
        grid=(1,),
        in_specs=[tail, pl.BlockSpec(memory_space=pl.ANY)] + [_resident(a.shape) for a in params],
        out_specs=(tail, _resident(h0_re.shape), _resident(h0_re.shape)),
        scratch_shapes=[pltpu.VMEM((rows, n_slab * LANES), F32),
                        pltpu.VMEM((rows, 2 * n_state), F32)],
        input_output_aliases={1: 0},
        compiler_params=_cparams("arbitrary"),
        name="s5_sample",
    )(u_slab, y_slab, *params)


def _subln(o, g_ref, post_scale):
    return _rms(o, g_ref[...], SUBLN_EPS) * post_scale


def _attn_prompt_kernel(lam_ref, q_ref, k_ref, v_ref, g_ref, o_ref, q2_scr, m_scr, l_scr, acc_scr,
                        *, blk, post_scale):
    qi = pl.program_id(2)
    q = q_ref[...]
    lane = lax.broadcasted_iota(jnp.int32, q.shape, 1)
    zero = jnp.zeros_like(q)
    q2_scr[0:blk] = jnp.where(lane < HEAD_DIM, q, zero)
    q2_scr[blk:2 * blk] = jnp.where(lane >= HEAD_DIM, q, zero)
    m_scr[...] = jnp.full_like(m_scr, NEG_BIG)
    l_scr[...] = jnp.zeros_like(l_scr)
    acc_scr[...] = jnp.zeros_like(acc_scr)

    n_rep = blk // LANES

    def block(kj, masked):
        r = pl.ds(pl.multiple_of(kj * blk, blk), blk)
        s = lax.dot_general(q2_scr[...], k_ref[r, :], (((1,), (1,)), ((), ())),
                            preferred_element_type=F32)
        if masked:
            qpos = lax.broadcasted_iota(jnp.int32, s.shape, 0) % blk
            kpos = lax.broadcasted_iota(jnp.int32, s.shape, 1)
            s = jnp.where(kpos <= qpos, s, NEG_BIG)
        m_old = m_scr[...]
        m_new = jnp.maximum(m_old, jnp.max(s, axis=-1, keepdims=True))
        alpha = jnp.exp2(m_old - m_new)
        p = jnp.exp2(s - jnp.concatenate([m_new] * n_rep, axis=-1))
        l_scr[...] = alpha * l_scr[...] + jnp.sum(p, axis=-1, keepdims=True)
        acc_scr[...] = alpha * acc_scr[...] + jnp.dot(p.astype(BF16), v_ref[r, :],
                                                      preferred_element_type=F32)
        m_scr[...] = m_new

    def full_block(kj, carry):
        block(kj, False)
        return carry

    lax.fori_loop(0, qi, full_block, 0)
    block(qi, True)

    o1 = acc_scr[0:blk] / l_scr[0:blk]
    o2 = acc_scr[blk:2 * blk] / l_scr[blk:2 * blk]
    o_ref[...] = _subln(o1 - lam_ref[0, 0] * o2, g_ref, post_scale)


def _attn_prompt(lam, q, k, v, g_subln, n_batch, seq, post_scale):
    blk = _largest_tile(seq, 512)
    nq = seq // blk
    q_spec = pl.BlockSpec((blk, V_DIM), lambda b, h, i: (b * nq + i, h))
    kv_spec = pl.BlockSpec((seq, V_DIM), lambda b, h, i: (b, h))
    return pl.pallas_call(
        functools.partial(_attn_prompt_kernel, blk=blk, post_scale=post_scale),
        out_shape=jax.ShapeDtypeStruct((q.shape[0], N_HEADS * V_DIM), F32),
        grid=(n_batch, N_HEADS, nq),
        in_specs=[pl.BlockSpec(memory_space=pltpu.SMEM), q_spec, kv_spec, kv_spec,
                  pl.BlockSpec((1, V_DIM), lambda b, h, i: (0, 0))],
        out_specs=q_spec,
        scratch_shapes=[pltpu.VMEM((2 * blk, V_DIM), BF16),
                        pltpu.VMEM((2 * blk, LANES), F32),
                        pltpu.VMEM((2 * blk, LANES), F32),
                        pltpu.VMEM((2 * blk, V_DIM), F32)],
        compiler_params=_cparams("arbitrary", "arbitrary", "arbitrary"),
        name="attn_prompt",
    )(lam, q, k, v, g_subln)


def _attn_sample_kernel(pt_ref, lam_ref, q_ref, kn_ref, vn_ref, g_ref, o_alias_ref, *rest,
                        n_pages, post_scale):
    del pt_ref, o_alias_ref
    kt_pages = rest[:n_pages]
    v_pages = rest[n_pages:2 * n_pages]
    o_ref = rest[2 * n_pages]
    kt_scr, v_scr = rest[2 * n_pages + 1:]
    width = q_ref.shape[1]
    n_row = N_HEADS * 2 * DEC_SEQ

    for j in range(n_pages):
        cols = slice(j * PAGE_SIZE, (j + 1) * PAGE_SIZE)
        kt_scr[:, cols] = kt_pages[j][...].astype(BF16)
        for h in range(N_HEADS):
            v_scr[h, cols, :] = v_pages[j][pl.ds(h, PAGE_SIZE, stride=N_HEADS), :].astype(BF16)
    def pad_rows(x):
        pad = jnp.zeros((PAGE_SIZE - DEC_SEQ, x.shape[1]), F32)
        return jnp.concatenate([x, pad], axis=0).astype(BF16)

    k_new = pad_rows(kn_ref[...])

    qt = jnp.concatenate([q_ref[...]] * (n_row // DEC_SEQ), axis=0)
    rid = lax.broadcasted_iota(jnp.int32, qt.shape, 0)
    blk_id = lax.broadcasted_iota(jnp.int32, qt.shape, 1) // HEAD_DIM
    qm = jnp.where(blk_id == rid // DEC_SEQ, qt, 0.0).astype(BF16)

    s = jnp.dot(qm, kt_scr[...], preferred_element_type=F32)
    s_new = lax.dot_general(qm, k_new, (((1,), (1,)), ((), ())), preferred_element_type=F32)
    t_new = lax.broadcasted_iota(jnp.int32, s_new.shape, 1)
    q_idx = lax.broadcasted_iota(jnp.int32, s_new.shape, 0) % DEC_SEQ
    s_new = jnp.where(t_new <= q_idx, s_new, NEG_BIG)

    m = jnp.maximum(jnp.max(s, axis=-1, keepdims=True), jnp.max(s_new, axis=-1, keepdims=True))
    p = jnp.exp2(s - m)
    p_new = jnp.exp2(s_new - m)
    l = jnp.sum(p, axis=-1, keepdims=True) + jnp.sum(p_new, axis=-1, keepdims=True)
    pb = p.astype(BF16)
    pb_new = p_new.astype(BF16)
    lam = lam_ref[0, 0]
    for h in range(N_HEADS):
        rows = slice(2 * DEC_SEQ * h, 2 * DEC_SEQ * (h + 1))
        v_new = pad_rows(vn_ref[pl.ds(h, DEC_SEQ, stride=N_HEADS), :])
        o_h = (jnp.dot(pb[rows], v_scr[h], preferred_element_type=F32)
               + jnp.dot(pb_new[rows], v_new, preferred_element_type=F32)) / l[rows]
        o_ref[:, h * V_DIM:(h + 1) * V_DIM] = _subln(
            o_h[0:DEC_SEQ] - lam * o_h[DEC_SEQ:2 * DEC_SEQ], g_ref, post_scale)


def _attn_sample(page_table, lam, q, k, v, o, g_subln, kt_pool, v_pool, layer, post_scale):
    n_seq, n_pages = page_table.shape
    t, width = q.shape
    past = n_pages * PAGE_SIZE
    first = t // DEC_SEQ - n_seq
    tok = pl.BlockSpec((DEC_SEQ, width), lambda n, pt: (first + n, 0))
    new_k = pl.BlockSpec((None, DEC_SEQ, width), lambda n, pt: (layer, n, 0))
    new_v = pl.BlockSpec((None, DEC_SEQ * N_HEADS, V_DIM), lambda n, pt: (layer, n, 0))

    def page_spec(shape, j):
        return pl.BlockSpec((None, None) + shape, lambda n, pt, j=j: (layer, pt[n, j], 0, 0))

    kt_specs = [page_spec(kt_pool.shape[2:], j) for j in range(n_pages)]
    v_specs = [page_spec(v_pool.shape[2:], j) for j in range(n_pages)]
    grid_spec = pltpu.PrefetchScalarGridSpec(
        num_scalar_prefetch=1,
        grid=(n_seq,),
        in_specs=[pl.BlockSpec(memory_space=pltpu.SMEM), tok, new_k, new_v,
                  pl.BlockSpec((1, V_DIM), lambda n, pt: (0, 0)),
                  pl.BlockSpec(memory_space=pl.ANY)] + kt_specs + v_specs,
        out_specs=tok,
        scratch_shapes=[pltpu.VMEM((width, past), BF16),
                        pltpu.VMEM((N_HEADS, past, V_DIM), BF16)],
    )
    return pl.pallas_call(
        functools.partial(_attn_sample_kernel, n_pages=n_pages, post_scale=post_scale),
        out_shape=jax.ShapeDtypeStruct(o.shape, F32),
        grid_spec=grid_spec,
        input_output_aliases={6: 0},
        compiler_params=_cparams("arbitrary"),
        name="attn_sample",
    )(page_table, lam, q, k, v, g_subln, o, *([kt_pool] * n_pages), *([v_pool] * n_pages))


def _merge_kernel(x_ref, ys_ref, ya_ref, gate_ref, wbs_ref, wba_ref, wo_ref, g_ref, o_ref):
    d = x_ref.shape[1]
    ys = jnp.concatenate([ys_ref[j] for j in range(ys_ref.shape[0])], axis=-1).astype(BF16)
    ya = ya_ref[...].astype(BF16)
    gate = gate_ref[...]
    merged = (jax.nn.sigmoid(gate[:, :d]) * jnp.dot(ys, wbs_ref[...], preferred_element_type=F32)
              + jax.nn.sigmoid(gate[:, d:]) * jnp.dot(ya, wba_ref[...], preferred_element_type=F32))
    out = jnp.dot(merged.astype(BF16), wo_ref[...], preferred_element_type=F32)
    o_ref[...] = x_ref[...] + _rms(out, g_ref[...], RMS_EPS)


def _merge(x, y_s5, y_att, gate, wbs, wba, wo, g_post):
    t, d = x.shape
    tm = _largest_tile(t, 512)
    n_slab = y_s5.shape[0]

    def rows(width):
        return pl.BlockSpec((tm, width), lambda i: (i, 0))

    return pl.pallas_call(
        _merge_kernel,
        out_shape=jax.ShapeDtypeStruct((t, d), F32),
        grid=(t // tm,),
        in_specs=[rows(d), pl.BlockSpec((n_slab, tm, LANES), lambda i: (0, i, 0)),
                  rows(y_att.shape[1]), rows(gate.shape[1]),
                  _resident(wbs.shape), _resident(wba.shape), _resident(wo.shape),
                  pl.BlockSpec((1, d), lambda i: (0, 0))],
        out_specs=rows(d),
        compiler_params=_cparams("arbitrary"),
        name="merge",
    )(x, y_s5, y_att, gate, wbs, wba, wo, g_post)


def _block_diag(m):
    eye = jnp.eye(GROUPS_PER_SLAB, dtype=m.dtype)
    depth, n_slab, g, r, c = m.shape
    return jnp.einsum("ljgrc,gh->ljgrhc", m, eye).reshape(depth, n_slab, g * r, g * c)


def _s5_params(a_re, a_im, log_dt, b_re, b_im, c_re, c_im, d, w_glu, tc):
    depth, groups, n_p = a_re.shape
    n_slab = groups // GROUPS_PER_SLAB
    seg = tc // SUBLANES
    dt = jnp.exp(log_dt)[:, :, None]
    mag = jnp.exp(dt * a_re)
    abar_re, abar_im = mag * jnp.cos(dt * a_im), mag * jnp.sin(dt * a_im)
    den = a_re * a_re + a_im * a_im
    inv_re, inv_im = a_re / den, -a_im / den
    fac_re = (abar_re - 1.0) * inv_re - abar_im * inv_im
    fac_im = (abar_re - 1.0) * inv_im + abar_im * inv_re
    fb_re = fac_re[..., None] * b_re - fac_im[..., None] * b_im
    fb_im = fac_re[..., None] * b_im + fac_im[..., None] * b_re

    def slab(x):
        return x.reshape((depth, n_slab, GROUPS_PER_SLAB) + x.shape[2:])

    bbd = jnp.concatenate([_block_diag(slab(fb_re).swapaxes(-1, -2)),
                           _block_diag(slab(fb_im).swapaxes(-1, -2))], axis=-1).astype(BF16)
    cbd = jnp.concatenate([_block_diag(slab(c_re).swapaxes(-1, -2)),
                           _block_diag(slab(-c_im).swapaxes(-1, -2))], axis=-2).astype(BF16)

    def lanes(x):
        return x.reshape(depth, n_slab, SLAB_STATES)

    k = jnp.arange(1, seg + 1, dtype=F32)[None, :, None, None]
    pw_mag = jnp.exp(k * (dt * a_re)[:, None])
    pw_arg = k * (dt * a_im)[:, None]
    pw_re, pw_im = pw_mag * jnp.cos(pw_arg), pw_mag * jnp.sin(pw_arg)

    def rows8(x):
        return jnp.broadcast_to(x[..., None, :], x.shape[:-1] + (SUBLANES, x.shape[-1]))

    def seg_lanes(x):
        return x.reshape(depth, seg, n_slab, SLAB_STATES).swapaxes(1, 2)

    ab = rows8(jnp.stack([lanes(abar_re), lanes(abar_im)], axis=2))
    aps = rows8(jnp.stack([lanes(pw_re[:, -1]), lanes(pw_im[:, -1])], axis=2))
    pw = rows8(jnp.stack([seg_lanes(pw_re), seg_lanes(pw_im)], axis=2))
    return dict(bbd=bbd, cbd=cbd, ab=ab, aps=aps, pw=pw,
                d=d[:, None, :], wglu=w_glu.astype(BF16))


def kernel(x_prompt, x_sample, cache_k, cache_v, state_s5_re, state_s5_im, page_table, g_ffn1_pre, g_ffn1_post, w_ffn1_gate, w_ffn1_up, w_ffn1_down, g_mix_pre, g_mix_post, w_in, s5_a_re, s5_a_im, s5_log_dt, s5_b_re, s5_b_im, s5_c_re, s5_c_im, s5_d, w_glu, lambda_q1, lambda_k1, lambda_q2, lambda_k2, g_subln, w_branch_s5, w_branch_att, w_out, g_ffn2_pre, g_ffn2_post, w_ffn2_gate, w_ffn2_up, w_ffn2_down):
    n_batch, seq, d_model = x_prompt.shape
    n_seq = x_sample.shape[0]
    depth = w_in.shape[0]
    n_prompt = n_batch * seq
    groups, n_p = s5_a_re.shape[1:]
    d_s5 = groups * S5_GROUP
    d_qk = N_HEADS * 2 * HEAD_DIM
    d_att = N_HEADS * V_DIM
    n_state = groups * n_p

    x = jnp.concatenate([x_prompt.reshape(n_prompt, d_model),
                         x_sample.reshape(n_seq * DEC_SEQ, d_model)], axis=0)

    tc = _largest_tile(seq, 256)
    ffn1 = [w.astype(BF16) for w in (w_ffn1_gate, w_ffn1_up, w_ffn1_down)]
    ffn2 = [w.astype(BF16) for w in (w_ffn2_gate, w_ffn2_up, w_ffn2_down)]
    w_in_b = w_in.astype(BF16)
    wbs_b, wba_b, wo_b = w_branch_s5.astype(BF16), w_branch_att.astype(BF16), w_out.astype(BF16)
    s5p = _s5_params(s5_a_re, s5_a_im, s5_log_dt, s5_b_re, s5_b_im, s5_c_re, s5_c_im,
                     s5_d, w_glu, tc)
    s5p["tc"] = tc
    lam_dyn = (jnp.exp(jnp.sum(lambda_q1 * lambda_k1, axis=-1))
               - jnp.exp(jnp.sum(lambda_q2 * lambda_k2, axis=-1)))
    n_phys = cache_k.shape[1]
    kt_pool = cache_k.transpose(0, 1, 3, 4, 5, 2).reshape(depth, n_phys, d_qk, PAGE_SIZE)
    v_pool = cache_v.reshape(depth, n_phys, PAGE_SIZE * N_HEADS, V_DIM)
    h0_re = state_s5_re.reshape(depth, n_seq, n_state)
    h0_im = state_s5_im.reshape(depth, n_seq, n_state)

    def vec(g, l):
        return g[l][None, :]

    stacks = ()
    s5_states = [[] for _ in range(4)]
    for l in range(depth):
        lambda_init = 0.8 - 0.6 * math.exp(-0.3 * l)
        lam = (lam_dyn[l] + lambda_init).reshape(1, 1)
        p_l = {k: (v if k == "tc" else v[l]) for k, v in s5p.items()}

        x = _ffn(x, vec(g_ffn1_pre, l), vec(g_ffn1_post, l), *(w[l] for w in ffn1))
        u, q_f, q_b, k_b, v_b, gate, *stacks = _inproj(
            x, vec(g_mix_pre, l), w_in_b[l], stacks, l, depth, n_batch, seq, d_s5, d_qk, d_att)
        kt_p, k_s, v_p, v_s = stacks

        y, sp_re, sp_im = _s5_prompt(u, n_batch, seq, p_l)
        y, ss_re, ss_im = _s5_sample(u, y, h0_re[l], h0_im[l], p_l)
        g_sub = vec(g_subln, l)
        o = _attn_prompt(lam, q_b, k_b, v_b, g_sub, n_batch, seq, 1.0 - lambda_init)
        o = _attn_sample(page_table, lam, q_f, k_s, v_s, o, g_sub, kt_pool, v_pool, l,
                         1.0 - lambda_init)
        x = _merge(x, y, o, gate, wbs_b[l], wba_b[l], wo_b[l], vec(g_mix_post, l))
        x = _ffn(x, vec(g_ffn2_pre, l), vec(g_ffn2_post, l), *(w[l] for w in ffn2))

        for lst, val in zip(s5_states, (sp_re, sp_im, ss_re, ss_im)):
            lst.append(val)

    sp_re, sp_im, ss_re, ss_im = [jnp.stack(s) for s in s5_states]
    k_p = kt_p.reshape(depth, n_batch, N_HEADS, 2, HEAD_DIM, seq).transpose(0, 1, 5, 2, 3, 4)
    return (x[:n_prompt].reshape(n_batch, seq, d_model),
            x[n_prompt:].reshape(n_seq, DEC_SEQ, d_model),
            k_p,
            v_p.reshape(depth, n_batch, seq, N_HEADS, V_DIM),
            sp_re.reshape(depth, n_batch, groups, n_p),
            sp_im.reshape(depth, n_batch, groups, n_p),
            k_s.reshape(depth, n_seq, DEC_SEQ, N_HEADS, 2, HEAD_DIM),
            v_s.reshape(depth, n_seq, DEC_SEQ, N_HEADS, V_DIM),
            ss_re.reshape(depth, n_seq, groups, n_p),
            ss_im.reshape(depth, n_seq, groups, n_p))
```

```python
import functools
import math

import jax
import jax.numpy as jnp
from jax import lax
from jax.experimental import pallas as pl
from jax.experimental.pallas import tpu as pltpu

F32 = jnp.float32
BF16 = jnp.bfloat16

PAGE_SIZE = 128
DEC_SEQ = 8
S5_GROUP = 16
S5_STATE = 64
N_HEADS = 4
HEAD_DIM = 64
V_DIM = 2 * HEAD_DIM
RMS_EPS = 1e-6
SUBLN_EPS = 1e-5
NEG_BIG = -1e30
Q_SCALE = HEAD_DIM ** -0.5 * math.log2(math.e)

LANES = 128
SUBLANES = 8
GROUPS_PER_SLAB = LANES // S5_GROUP
SLAB_STATES = GROUPS_PER_SLAB * S5_STATE
FF_CHUNK = 256
VMEM_LIMIT = 56 * 1024 * 1024


def _largest_tile(n, pref):
    t = pref
    while n % t:
        t //= 2
    return t


def _rms(x, g, eps):
    return x * lax.rsqrt(jnp.mean(x * x, axis=-1, keepdims=True) + eps) * g


def _cparams(*sem):
    return pltpu.CompilerParams(dimension_semantics=sem, vmem_limit_bytes=VMEM_LIMIT)


def _resident(shape):
    nd = len(shape)
    return pl.BlockSpec(shape, lambda *_: (0,) * nd, pipeline_mode=pl.Buffered(1))


def _layer_resident(stack, layer, block=None, index=None):
    block = tuple(stack.shape[1:]) if block is None else block
    index = (0,) * len(block) if index is None else index
    return pl.BlockSpec((None,) + block, lambda *_: (layer,) + index, pipeline_mode=pl.Buffered(1))


def _ffn_kernel(x_ref, gpre_ref, gpost_ref, wg_ref, wu_ref, wd_ref, o_ref, h_scr, acc_scr):
    x = x_ref[...]
    h_scr[...] = _rms(x, gpre_ref[...], RMS_EPS).astype(BF16)
    for c in range(wg_ref.shape[1] // FF_CHUNK):
        cols = slice(c * FF_CHUNK, (c + 1) * FF_CHUNK)
        g = jnp.dot(h_scr[...], wg_ref[:, cols], preferred_element_type=F32)
        u = jnp.dot(h_scr[...], wu_ref[:, cols], preferred_element_type=F32)
        a = (g * jax.nn.sigmoid(g) * u).astype(BF16)
        part = jnp.dot(a, wd_ref[cols, :], preferred_element_type=F32)
        if c == 0:
            acc_scr[...] = part
        else:
            acc_scr[...] += part
    o_ref[...] = x + 0.5 * _rms(acc_scr[...], gpost_ref[...], RMS_EPS)


def _ffn(x, gpre, gpost, wg, wu, wd, layer):
    t, d = x.shape
    tm = _largest_tile(t, 1024)
    row = pl.BlockSpec((tm, d), lambda i: (i, 0))
    vec = pl.BlockSpec((1, d), lambda i: (0, 0))
    return pl.pallas_call(
        _ffn_kernel,
        out_shape=jax.ShapeDtypeStruct((t, d), F32),
        grid=(t // tm,),
        in_specs=[row, vec, vec] + [_layer_resident(w, layer) for w in (wg, wu, wd)],
        out_specs=row,
        scratch_shapes=[pltpu.VMEM((tm, d), BF16), pltpu.VMEM((tm, d), F32)],
        compiler_params=_cparams("arbitrary"),
        name="ffn",
    )(x, gpre, gpost, wg, wu, wd)


def _inproj_kernel(x_ref, g_ref, w_ref, *rest, d_s5, d_qk, d_att, n_prompt_tiles):
    u_ref, qf_ref, qb_ref, kb_ref, vb_ref, ktp_ref, ks_ref, vp_ref, vs_ref = rest[-9:]
    i = pl.program_id(0)
    h = _rms(x_ref[...], g_ref[...], RMS_EPS).astype(BF16)

    def proj(lo, width):
        return jnp.dot(h, w_ref[:, lo:lo + width], preferred_element_type=F32)

    u = proj(0, d_s5)
    for j in range(d_s5 // LANES):
        u_ref[j] = u[:, j * LANES:(j + 1) * LANES]
    q = proj(d_s5, d_qk) * Q_SCALE
    qf_ref[...] = q
    qb_ref[...] = q.astype(BF16)
    k = proj(d_s5 + d_qk, d_qk)
    kb_ref[...] = k.astype(BF16)
    v = proj(d_s5 + 2 * d_qk, d_att)
    vb_ref[...] = v.astype(BF16)

    def store_heads(v_ref):
        for hd in range(N_HEADS):
            v_ref[pl.ds(hd, v.shape[0], stride=N_HEADS), :] = v[:, hd * V_DIM:(hd + 1) * V_DIM]

    @pl.when(i < n_prompt_tiles)
    def _():
        ktp_ref[...] = k.T
        store_heads(vp_ref)

    @pl.when(i >= n_prompt_tiles)
    def _():
        ks_ref[...] = k
        store_heads(vs_ref)


def _inproj(x, g, w, stacks, layer, depth, n_batch, seq, d_s5, d_qk, d_att):
    t, d = x.shape
    tm = _largest_tile(math.gcd(seq, t - n_batch * seq), 512)
    np_tiles = n_batch * seq // tm
    per_batch = seq // tm
    rows_s = t - n_batch * seq
    d_proj = d_s5 + 2 * d_qk + d_att

    def rows(width):
        return pl.BlockSpec((tm, width), lambda i: (i, 0))

    def p_tile(i):
        return jnp.minimum(i, np_tiles - 1)

    def s_tile(i):
        return jnp.maximum(i - np_tiles, 0)

    n_slab = d_s5 // LANES
    any_spec = pl.BlockSpec(memory_space=pl.ANY)
    n_in = 3
    return pl.pallas_call(
        functools.partial(_inproj_kernel, d_s5=d_s5, d_qk=d_qk, d_att=d_att,
                          n_prompt_tiles=np_tiles),
        out_shape=(jax.ShapeDtypeStruct((n_slab, t, LANES), F32),
                   jax.ShapeDtypeStruct((t, d_qk), F32),
                   jax.ShapeDtypeStruct((t, d_qk), BF16),
                   jax.ShapeDtypeStruct((t, d_qk), BF16),
                   jax.ShapeDtypeStruct((t, d_att), BF16),
                   jax.ShapeDtypeStruct((depth, n_batch, d_qk, seq), F32),
                   jax.ShapeDtypeStruct((depth, rows_s, d_qk), F32),
                   jax.ShapeDtypeStruct((depth, n_batch * seq * N_HEADS, V_DIM), F32),
                   jax.ShapeDtypeStruct((depth, rows_s * N_HEADS, V_DIM), F32)),
        grid=(t // tm,),
        in_specs=[rows(d), pl.BlockSpec((1, d), lambda i: (0, 0)),
                  _layer_resident(w, layer, block=(d, d_proj))] + [any_spec] * len(stacks),
        out_specs=(pl.BlockSpec((n_slab, tm, LANES), lambda i: (0, i, 0)),
                   rows(d_qk), rows(d_qk), rows(d_qk), rows(d_att),
                   pl.BlockSpec((None, None, d_qk, tm),
                                lambda i: (layer, p_tile(i) // per_batch, 0, p_tile(i) % per_batch)),
                   pl.BlockSpec((None, tm, d_qk), lambda i: (layer, s_tile(i), 0)),
                   pl.BlockSpec((None, tm * N_HEADS, V_DIM), lambda i: (layer, p_tile(i), 0)),
                   pl.BlockSpec((None, tm * N_HEADS, V_DIM), lambda i: (layer, s_tile(i), 0))),
        input_output_aliases={n_in + s: 5 + s for s in range(len(stacks))},
        compiler_params=_cparams("arbitrary"),
        name="in_proj",
    )(x, g, w, *stacks)


def _s5_readout(st_scr, up_scr, cbd_ref, d_ref, wglu_ref, n_slab):
    ys = []
    for j in range(n_slab):
        hj = st_scr[:, 2 * SLAB_STATES * j:2 * SLAB_STATES * (j + 1)].astype(BF16)
        ys.append(jnp.dot(hj, cbd_ref[j], preferred_element_type=F32))
    y = jnp.concatenate(ys, axis=-1) + d_ref[...] * up_scr[...]
    z = jax.nn.gelu(y).astype(BF16)
    zg = jnp.dot(z, wglu_ref[...], preferred_element_type=F32)
    half = zg.shape[1] // 2
    return zg[:, :half] * jax.nn.sigmoid(zg[:, half:])


def _s5_input(st_scr, up_scr, bbd_ref, n_slab):
    for j in range(n_slab):
        uj = up_scr[:, j * LANES:(j + 1) * LANES].astype(BF16)
        st_scr[:, 2 * SLAB_STATES * j:2 * SLAB_STATES * (j + 1)] = jnp.dot(
            uj, bbd_ref[j], preferred_element_type=F32)


def _s5_prompt_kernel(u_ref, bbd_ref, cbd_ref, ab_ref, aps_ref, pw_ref, d_ref, wglu_ref,
                      y_ref, sre_ref, sim_ref, up_scr, st_scr, carry_scr, *, tc):
    c = pl.program_id(1)
    n_slab = u_ref.shape[0]
    seg = tc // SUBLANES
    w = SLAB_STATES

    @pl.when(c == 0)
    def _():
        carry_scr[...] = jnp.zeros_like(carry_scr)

    for j in range(n_slab):
        for k in range(seg):
            up_scr[k * SUBLANES:(k + 1) * SUBLANES, j * LANES:(j + 1) * LANES] = (
                u_ref[j, pl.ds(k, SUBLANES, stride=seg), :])
    _s5_input(st_scr, up_scr, bbd_ref, n_slab)

    row_id = lax.broadcasted_iota(jnp.int32, (SUBLANES, w), 0)
    for j in range(n_slab):
        re = slice(2 * w * j, 2 * w * j + w)
        im = slice(2 * w * j + w, 2 * w * (j + 1))
        a_re = ab_ref[j, 0]
        a_im = ab_ref[j, 1]

        def scan_step(k, h, re=re, im=im, a_re=a_re, a_im=a_im):
            h_re, h_im = h
            r = pl.ds(pl.multiple_of(k * SUBLANES, SUBLANES), SUBLANES)
            n_re = a_re * h_re - a_im * h_im + st_scr[r, re]
            n_im = a_re * h_im + a_im * h_re + st_scr[r, im]
            st_scr[r, re] = n_re
            st_scr[r, im] = n_im
            return n_re, n_im

        zero = jnp.zeros((SUBLANES, w), F32)
        e_re, e_im = lax.fori_loop(0, seg, scan_step, (zero, zero), unroll=True)

        s_re = aps_ref[j, 0][0:1]
        s_im = aps_ref[j, 1][0:1]
        cur_re = carry_scr[0:1, re]
        cur_im = carry_scr[0:1, im]
        c_re = jnp.broadcast_to(cur_re, (SUBLANES, w))
        c_im = jnp.broadcast_to(cur_im, (SUBLANES, w))
        for i in range(1, SUBLANES + 1):
            nxt_re = s_re * cur_re - s_im * cur_im + e_re[i - 1:i]
            nxt_im = s_re * cur_im + s_im * cur_re + e_im[i - 1:i]
            cur_re, cur_im = nxt_re, nxt_im
            if i < SUBLANES:
                c_re = jnp.where(row_id == i, cur_re, c_re)
                c_im = jnp.where(row_id == i, cur_im, c_im)
        carry_scr[0:1, re] = cur_re
        carry_scr[0:1, im] = cur_im

        def fix_step(k, carry, re=re, im=im, c_re=c_re, c_im=c_im, j=j):
            r = pl.ds(pl.multiple_of(k * SUBLANES, SUBLANES), SUBLANES)
            p_re = pw_ref[j, 0, k]
            p_im = pw_ref[j, 1, k]
            st_scr[r, re] += p_re * c_re - p_im * c_im
            st_scr[r, im] += p_re * c_im + p_im * c_re
            return carry

        lax.fori_loop(0, seg, fix_step, 0, unroll=True)

    up_scr[...] = _s5_readout(st_scr, up_scr, cbd_ref, d_ref, wglu_ref, n_slab)
    for j in range(n_slab):
        for k in range(seg):
            y_ref[j, pl.ds(k, SUBLANES, stride=seg), :] = (
                up_scr[k * SUBLANES:(k + 1) * SUBLANES, j * LANES:(j + 1) * LANES])

    @pl.when(c == pl.num_programs(1) - 1)
    def _():
        for j in range(n_slab):
            sre_ref[0, :, j * w:(j + 1) * w] = carry_scr[0:1, 2 * w * j:2 * w * j + w]
            sim_ref[0, :, j * w:(j + 1) * w] = carry_scr[0:1, 2 * w * j + w:2 * w * (j + 1)]


def _s5_prompt(u_slab, n_batch, seq, p, layer):
    n_slab = u_slab.shape[0]
    d_s5 = n_slab * LANES
    tc = p["tc"]
    n_chunks = seq // tc
    n_state = n_slab * SLAB_STATES
    u_spec = pl.BlockSpec((n_slab, tc, LANES), lambda b, c: (0, b * n_chunks + c, 0))
    st_spec = pl.BlockSpec((1, 1, n_state), lambda b, c: (b, 0, 0))
    return pl.pallas_call(
        functools.partial(_s5_prompt_kernel, tc=tc),
        out_shape=(jax.ShapeDtypeStruct(u_slab.shape, F32),
                   jax.ShapeDtypeStruct((n_batch, 1, n_state), F32),
                   jax.ShapeDtypeStruct((n_batch, 1, n_state), F32)),
        grid=(n_batch, n_chunks),
        in_specs=[u_spec] + [_layer_resident(p[k], layer)
                             for k in ("bbd", "cbd", "ab", "aps", "pw", "d", "wglu")],
        out_specs=(u_spec, st_spec, st_spec),
        scratch_shapes=[pltpu.VMEM((tc, d_s5), F32),
                        pltpu.VMEM((tc, 2 * n_state), F32),
                        pltpu.VMEM((SUBLANES, 2 * n_state), F32)],
        compiler_params=_cparams("arbitrary", "arbitrary"),
        name="s5_prompt",
    )(u_slab, p["bbd"], p["cbd"], p["ab"], p["aps"], p["pw"], p["d"], p["wglu"])


def _s5_sample_kernel(u_ref, y_alias_ref, h0re_ref, h0im_ref, bbd_ref, cbd_ref, ab_ref, d_ref,
                      wglu_ref, y_ref, sre_ref, sim_ref, up_scr, st_scr):
    del y_alias_ref
    n_slab = u_ref.shape[0]
    n_seq = h0re_ref.shape[0]
    w = SLAB_STATES
    for j in range(n_slab):
        for t in range(DEC_SEQ):
            up_scr[t * n_seq:(t + 1) * n_seq, j * LANES:(j + 1) * LANES] = (
                u_ref[j, pl.ds(t, n_seq, stride=DEC_SEQ), :])
    _s5_input(st_scr, up_scr, bbd_ref, n_slab)

    for j in range(n_slab):
        re = slice(2 * w * j, 2 * w * j + w)
        im = slice(2 * w * j + w, 2 * w * (j + 1))
        a_re = ab_ref[j, 0]
        a_im = ab_ref[j, 1]

        def seq_group(g, carry, re=re, im=im, a_re=a_re, a_im=a_im, j=j):
            n0 = pl.multiple_of(g * SUBLANES, SUBLANES)
            h_re = h0re_ref[pl.ds(n0, SUBLANES), j * w:(j + 1) * w]
            h_im = h0im_ref[pl.ds(n0, SUBLANES), j * w:(j + 1) * w]
            for t in range(DEC_SEQ):
                r = pl.ds(pl.multiple_of(t * n_seq + n0, SUBLANES), SUBLANES)
                n_re = a_re * h_re - a_im * h_im + st_scr[r, re]
                n_im = a_re * h_im + a_im * h_re + st_scr[r, im]
                st_scr[r, re] = n_re
                st_scr[r, im] = n_im
                h_re, h_im = n_re, n_im
            sre_ref[pl.ds(n0, SUBLANES), j * w:(j + 1) * w] = h_re
            sim_ref[pl.ds(n0, SUBLANES), j * w:(j + 1) * w] = h_im
            return carry

        lax.fori_loop(0, n_seq // SUBLANES, seq_group, 0)

    up_scr[...] = _s5_readout(st_scr, up_scr, cbd_ref, d_ref, wglu_ref, n_slab)
    for j in range(n_slab):
        for t in range(DEC_SEQ):
            y_ref[j, pl.ds(t, n_seq, stride=DEC_SEQ), :] = (
                up_scr[t * n_seq:(t + 1) * n_seq, j * LANES:(j + 1) * LANES])


def _s5_sample(u_slab, y_slab, h0_re, h0_im, p, layer):
    n_slab, t, _ = u_slab.shape
    n_seq, n_state = h0_re.shape[1:]
    rows = n_seq * DEC_SEQ
    tail = pl.BlockSpec((n_slab, rows, LANES), lambda i: (0, (t - rows) // rows, 0))
    params = (h0_re, h0_im, p["bbd"], p["cbd"], p["ab"], p["d"], p["wglu"])
    return pl.pallas_call(
        _s5_sample_kernel,
        out_shape=(jax.ShapeDtypeStruct(y_slab.shape, F32),
                   jax.ShapeDtypeStruct((n_seq, n_state), F32),
                   jax.ShapeDtypeStruct((n_seq, n_state), F32)),
        grid=(1,),
        in_specs=[tail, pl.BlockSpec(memory_space=pl.ANY)]
        + [_layer_resident(a, layer) for a in params],
        out_specs=(tail, _resident((n_seq, n_state)), _resident((n_seq, n_state))),
        scratch_shapes=[pltpu.VMEM((rows, n_slab * LANES), F32),
                        pltpu.VMEM((rows, 2 * n_state), F32)],
        input_output_aliases={1: 0},
        compiler_params=_cparams("arbitrary"),
        name="s5_sample",
    )(u_slab, y_slab, *params)


def _subln(o, g_ref, post_scale):
    return _rms(o, g_ref[...], SUBLN_EPS) * post_scale


def _attn_prompt_kernel(lam_ref, q_ref, k_ref, v_ref, g_ref, o_ref, q2_scr, m_scr, l_scr, acc_scr,
                        *, blk, post_scale):
    qi = pl.program_id(2)
    q = q_ref[...]
    lane = lax.broadcasted_iota(jnp.int32, q.shape, 1)
    zero = jnp.zeros_like(q)
    q2_scr[0:blk] = jnp.where(lane < HEAD_DIM, q, zero)
    q2_scr[blk:2 * blk] = jnp.where(lane >= HEAD_DIM, q, zero)
    m_scr[...] = jnp.full_like(m_scr, NEG_BIG)
    l_scr[...] = jnp.zeros_like(l_scr)
    acc_scr[...] = jnp.zeros_like(acc_scr)

    n_rep = blk // LANES

    def block(kj, masked):
        r = pl.ds(pl.multiple_of(kj * blk, blk), blk)
        s = lax.dot_general(q2_scr[...], k_ref[r, :], (((1,), (1,)), ((), ())),
                            preferred_element_type=F32)
        if masked:
            qpos = lax.broadcasted_iota(jnp.int32, s.shape, 0) % blk
            kpos = lax.broadcasted_iota(jnp.int32, s.shape, 1)
            s = jnp.where(kpos <= qpos, s, NEG_BIG)
        m_old = m_scr[...]
        m_new = jnp.maximum(m_old, jnp.max(s, axis=-1, keepdims=True))
        alpha = jnp.exp2(m_old - m_new)
        p = jnp.exp2(s - jnp.concatenate([m_new] * n_rep, axis=-1))
        l_scr[...] = alpha * l_scr[...] + jnp.sum(p, axis=-1, keepdims=True)
        acc_scr[...] = alpha * acc_scr[...] + jnp.dot(p.astype(BF16), v_ref[r, :],
                                                      preferred_element_type=F32)
        m_scr[...] = m_new

    def full_block(kj, carry):
        block(kj, False)
        return carry

    lax.fori_loop(0, qi, full_block, 0)
    block(qi, True)

    o1 = acc_scr[0:blk] / l_scr[0:blk]
    o2 = acc_scr[blk:2 * blk] / l_scr[blk:2 * blk]
    o_ref[...] = _subln(o1 - lam_ref[0, 0] * o2, g_ref, post_scale)


def _attn_prompt(lam, q, k, v, g_subln, n_batch, seq, post_scale):
    blk = _largest_tile(seq, 512)
    nq = seq // blk
    q_spec = pl.BlockSpec((blk, V_DIM), lambda b, h, i: (b * nq + i, h))
    kv_spec = pl.BlockSpec((seq, V_DIM), lambda b, h, i: (b, h))
    return pl.pallas_call(
        functools.partial(_attn_prompt_kernel, blk=blk, post_scale=post_scale),
        out_shape=jax.ShapeDtypeStruct((q.shape[0], N_HEADS * V_DIM), F32),
        grid=(n_batch, N_HEADS, nq),
        in_specs=[pl.BlockSpec(memory_space=pltpu.SMEM), q_spec, kv_spec, kv_spec,
                  pl.BlockSpec((1, V_DIM), lambda b, h, i: (0, 0))],
        out_specs=q_spec,
        scratch_shapes=[pltpu.VMEM((2 * blk, V_DIM), BF16),
                        pltpu.VMEM((2 * blk, LANES), F32),
                        pltpu.VMEM((2 * blk, LANES), F32),
                        pltpu.VMEM((2 * blk, V_DIM), F32)],
        compiler_params=_cparams("arbitrary", "arbitrary", "arbitrary"),
        name="attn_prompt",
    )(lam, q, k, v, g_subln)


def _attn_sample_kernel(pt_ref, lam_ref, q_ref, kn_ref, vn_ref, g_ref, o_alias_ref, *rest,
                        n_pages, post_scale):
    del pt_ref, o_alias_ref
    kt_pages = rest[:n_pages]
    v_pages = rest[n_pages:2 * n_pages]
    o_ref = rest[2 * n_pages]
    kt_scr, v_scr = rest[2 * n_pages + 1:]
    width = q_ref.shape[1]
    n_row = N_HEADS * 2 * DEC_SEQ

    for j in range(n_pages):
        cols = slice(j * PAGE_SIZE, (j + 1) * PAGE_SIZE)
        kt_scr[:, cols] = kt_pages[j][...].astype(BF16)
        for h in range(N_HEADS):
            v_scr[h, cols, :] = v_pages[j][pl.ds(h, PAGE_SIZE, stride=N_HEADS), :].astype(BF16)
    def pad_rows(x):
        pad = jnp.zeros((PAGE_SIZE - DEC_SEQ, x.shape[1]), F32)
        return jnp.concatenate([x, pad], axis=0).astype(BF16)

    k_new = pad_rows(kn_ref[...])

    qt = jnp.concatenate([q_ref[...]] * (n_row // DEC_SEQ), axis=0)
    rid = lax.broadcasted_iota(jnp.int32, qt.shape, 0)
    blk_id = lax.broadcasted_iota(jnp.int32, qt.shape, 1) // HEAD_DIM
    qm = jnp.where(blk_id == rid // DEC_SEQ, qt, 0.0).astype(BF16)

    s = jnp.dot(qm, kt_scr[...], preferred_element_type=F32)
    s_new = lax.dot_general(qm, k_new, (((1,), (1,)), ((), ())), preferred_element_type=F32)
    t_new = lax.broadcasted_iota(jnp.int32, s_new.shape, 1)
    q_idx = lax.broadcasted_iota(jnp.int32, s_new.shape, 0) % DEC_SEQ
    s_new = jnp.where(t_new <= q_idx, s_new, NEG_BIG)

    m = jnp.maximum(jnp.max(s, axis=-1, keepdims=True), jnp.max(s_new, axis=-1, keepdims=True))
    p = jnp.exp2(s - m)
    p_new = jnp.exp2(s_new - m)
    l = jnp.sum(p, axis=-1, keepdims=True) + jnp.sum(p_new, axis=-1, keepdims=True)
    pb = p.astype(BF16)
    pb_new = p_new.astype(BF16)
    lam = lam_ref[0, 0]
    for h in range(N_HEADS):
        rows = slice(2 * DEC_SEQ * h, 2 * DEC_SEQ * (h + 1))
        v_new = pad_rows(vn_ref[pl.ds(h, DEC_SEQ, stride=N_HEADS), :])
        o_h = (jnp.dot(pb[rows], v_scr[h], preferred_element_type=F32)
               + jnp.dot(pb_new[rows], v_new, preferred_element_type=F32)) / l[rows]
        o_ref[:, h * V_DIM:(h + 1) * V_DIM] = _subln(
            o_h[0:DEC_SEQ] - lam * o_h[DEC_SEQ:2 * DEC_SEQ], g_ref, post_scale)


def _attn_sample(page_table, lam, q, k, v, o, g_subln, kt_pool, v_pool, layer, post_scale):
    n_seq, n_pages = page_table.shape
    t, width = q.shape
    past = n_pages * PAGE_SIZE
    first = t // DEC_SEQ - n_seq
    tok = pl.BlockSpec((DEC_SEQ, width), lambda n, pt: (first + n, 0))
    new_k = pl.BlockSpec((None, DEC_SEQ, width), lambda n, pt: (layer, n, 0))
    new_v = pl.BlockSpec((None, DEC_SEQ * N_HEADS, V_DIM), lambda n, pt: (layer, n, 0))

    def page_spec(shape, j):
        return pl.BlockSpec((None, None) + shape, lambda n, pt, j=j: (layer, pt[n, j], 0, 0))

    kt_specs = [page_spec(kt_pool.shape[2:], j) for j in range(n_pages)]
    v_specs = [page_spec(v_pool.shape[2:], j) for j in range(n_pages)]
    grid_spec = pltpu.PrefetchScalarGridSpec(
        num_scalar_prefetch=1,
        grid=(n_seq,),
        in_specs=[pl.BlockSpec(memory_space=pltpu.SMEM), tok, new_k, new_v,
                  pl.BlockSpec((1, V_DIM), lambda n, pt: (0, 0)),
                  pl.BlockSpec(memory_space=pl.ANY)] + kt_specs + v_specs,
        out_specs=tok,
        scratch_shapes=[pltpu.VMEM((width, past), BF16),
                        pltpu.VMEM((N_HEADS, past, V_DIM), BF16)],
    )
    return pl.pallas_call(
        functools.partial(_attn_sample_kernel, n_pages=n_pages, post_scale=post_scale),
        out_shape=jax.ShapeDtypeStruct(o.shape, F32),
        grid_spec=grid_spec,
        input_output_aliases={6: 0},
        compiler_params=_cparams("arbitrary"),
        name="attn_sample",
    )(page_table, lam, q, k, v, g_subln, o, *([kt_pool] * n_pages), *([v_pool] * n_pages))


def _merge_kernel(x_ref, ys_ref, ya_ref, gpre_ref, wgate_ref, wbs_ref, wba_ref, wo_ref, g_ref,
                  o_ref):
    d = x_ref.shape[1]
    ys = jnp.concatenate([ys_ref[j] for j in range(ys_ref.shape[0])], axis=-1).astype(BF16)
    ya = ya_ref[...].astype(BF16)
    h = _rms(x_ref[...], gpre_ref[...], RMS_EPS).astype(BF16)
    gate = jnp.dot(h, wgate_ref[...], preferred_element_type=F32)
    merged = (jax.nn.sigmoid(gate[:, :d]) * jnp.dot(ys, wbs_ref[...], preferred_element_type=F32)
              + jax.nn.sigmoid(gate[:, d:]) * jnp.dot(ya, wba_ref[...], preferred_element_type=F32))
    out = jnp.dot(merged.astype(BF16), wo_ref[...], preferred_element_type=F32)
    o_ref[...] = x_ref[...] + _rms(out, g_ref[...], RMS_EPS)


def _merge(x, y_s5, y_att, g_pre, w_in, wbs, wba, wo, g_post, layer):
    t, d = x.shape
    tm = _largest_tile(t, 512)
    n_slab = y_s5.shape[0]

    def rows(width):
        return pl.BlockSpec((tm, width), lambda i: (i, 0))

    vec = pl.BlockSpec((1, d), lambda i: (0, 0))
    return pl.pallas_call(
        _merge_kernel,
        out_shape=jax.ShapeDtypeStruct((t, d), F32),
        grid=(t // tm,),
        in_specs=[rows(d), pl.BlockSpec((n_slab, tm, LANES), lambda i: (0, i, 0)),
                  rows(y_att.shape[1]), vec,
                  _layer_resident(w_in, layer, block=(d, 2 * d), index=(0, w_in.shape[2] // (2 * d) - 1)),
                  _layer_resident(wbs, layer), _layer_resident(wba, layer),
                  _layer_resident(wo, layer), vec],
        out_specs=rows(d),
        compiler_params=_cparams("arbitrary"),
        name="merge",
    )(x, y_s5, y_att, g_pre, w_in, wbs, wba, wo, g_post)


def _block_diag(m):
    eye = jnp.eye(GROUPS_PER_SLAB, dtype=m.dtype)
    depth, n_slab, g, r, c = m.shape
    return jnp.einsum("ljgrc,gh->ljgrhc", m, eye).reshape(depth, n_slab, g * r, g * c)


def _s5_params(a_re, a_im, log_dt, b_re, b_im, c_re, c_im, d, w_glu, tc):
    depth, groups, n_p = a_re.shape
    n_slab = groups // GROUPS_PER_SLAB
    seg = tc // SUBLANES
    dt = jnp.exp(log_dt)[:, :, None]
    mag = jnp.exp(dt * a_re)
    abar_re, abar_im = mag * jnp.cos(dt * a_im), mag * jnp.sin(dt * a_im)
    den = a_re * a_re + a_im * a_im
    inv_re, inv_im = a_re / den, -a_im / den
    fac_re = (abar_re - 1.0) * inv_re - abar_im * inv_im
    fac_im = (abar_re - 1.0) * inv_im + abar_im * inv_re
    fb_re = fac_re[..., None] * b_re - fac_im[..., None] * b_im
    fb_im = fac_re[..., None] * b_im + fac_im[..., None] * b_re

    def slab(x):
        return x.reshape((depth, n_slab, GROUPS_PER_SLAB) + x.shape[2:])

    bbd = jnp.concatenate([_block_diag(slab(fb_re).swapaxes(-1, -2)),
                           _block_diag(slab(fb_im).swapaxes(-1, -2))], axis=-1).astype(BF16)
    cbd = jnp.concatenate([_block_diag(slab(c_re).swapaxes(-1, -2)),
                           _block_diag(slab(-c_im).swapaxes(-1, -2))], axis=-2).astype(BF16)

    def lanes(x):
        return x.reshape(depth, n_slab, SLAB_STATES)

    k = jnp.arange(1, seg + 1, dtype=F32)[None, :, None, None]
    pw_mag = jnp.exp(k * (dt * a_re)[:, None])
    pw_arg = k * (dt * a_im)[:, None]
    pw_re, pw_im = pw_mag * jnp.cos(pw_arg), pw_mag * jnp.sin(pw_arg)

    def rows8(x):
        return jnp.broadcast_to(x[..., None, :], x.shape[:-1] + (SUBLANES, x.shape[-1]))

    def seg_lanes(x):
        return x.reshape(depth, seg, n_slab, SLAB_STATES).swapaxes(1, 2)

    ab = rows8(jnp.stack([lanes(abar_re), lanes(abar_im)], axis=2))
    aps = rows8(jnp.stack([lanes(pw_re[:, -1]), lanes(pw_im[:, -1])], axis=2))
    pw = rows8(jnp.stack([seg_lanes(pw_re), seg_lanes(pw_im)], axis=2))
    return dict(bbd=bbd, cbd=cbd, ab=ab, aps=aps, pw=pw,
                d=d[:, None, :], wglu=w_glu.astype(BF16))


def kernel(x_prompt, x_sample, cache_k, cache_v, state_s5_re, state_s5_im, page_table, g_ffn1_pre, g_ffn1_post, w_ffn1_gate, w_ffn1_up, w_ffn1_down, g_mix_pre, g_mix_post, w_in, s5_a_re, s5_a_im, s5_log_dt, s5_b_re, s5_b_im, s5_c_re, s5_c_im, s5_d, w_glu, lambda_q1, lambda_k1, lambda_q2, lambda_k2, g_subln, w_branch_s5, w_branch_att, w_out, g_ffn2_pre, g_ffn2_post, w_ffn2_gate, w_ffn2_up, w_ffn2_down):
    n_batch, seq, d_model = x_prompt.shape
    n_seq = x_sample.shape[0]
    depth = w_in.shape[0]
    n_prompt = n_batch * seq
    groups, n_p = s5_a_re.shape[1:]
    d_s5 = groups * S5_GROUP
    d_qk = N_HEADS * 2 * HEAD_DIM
    d_att = N_HEADS * V_DIM
    n_state = groups * n_p

    x = jnp.concatenate([x_prompt.reshape(n_prompt, d_model),
                         x_sample.reshape(n_seq * DEC_SEQ, d_model)], axis=0)

    tc = _largest_tile(seq, 256)
    ffn1 = [w.astype(BF16) for w in (w_ffn1_gate, w_ffn1_up, w_ffn1_down)]
    ffn2 = [w.astype(BF16) for w in (w_ffn2_gate, w_ffn2_up, w_ffn2_down)]
    w_in_b = w_in.astype(BF16)
    wbs_b, wba_b, wo_b = w_branch_s5.astype(BF16), w_branch_att.astype(BF16), w_out.astype(BF16)
    s5p = _s5_params(s5_a_re, s5_a_im, s5_log_dt, s5_b_re, s5_b_im, s5_c_re, s5_c_im,
                     s5_d, w_glu, tc)
    s5p["tc"] = tc
    lam_dyn = (jnp.exp(jnp.sum(lambda_q1 * lambda_k1, axis=-1))
               - jnp.exp(jnp.sum(lambda_q2 * lambda_k2, axis=-1)))
    n_phys = cache_k.shape[1]
    kt_pool = cache_k.transpose(0, 1, 3, 4, 5, 2).reshape(depth, n_phys, d_qk, PAGE_SIZE)
    v_pool = cache_v.reshape(depth, n_phys, PAGE_SIZE * N_HEADS, V_DIM)
    h0_re = state_s5_re.reshape(depth, n_seq, n_state)
    h0_im = state_s5_im.reshape(depth, n_seq, n_state)
    assert w_in.shape[2] == d_s5 + 2 * d_qk + d_att + 2 * d_model == 2 * (2 * d_model)

    def vec(g, l):
        return g[l][None, :]

    stacks = ()
    s5_states = [[] for _ in range(4)]
    for l in range(depth):
        lambda_init = 0.8 - 0.6 * math.exp(-0.3 * l)
        lam = (lam_dyn[l] + lambda_init).reshape(1, 1)

        x = _ffn(x, vec(g_ffn1_pre, l), vec(g_ffn1_post, l), *ffn1, l)
        u, q_f, q_b, k_b, v_b, *stacks = _inproj(
            x, vec(g_mix_pre, l), w_in_b, stacks, l, depth, n_batch, seq, d_s5, d_qk, d_att)
        kt_p, k_s, v_p, v_s = stacks

        y, sp_re, sp_im = _s5_prompt(u, n_batch, seq, s5p, l)
        y, ss_re, ss_im = _s5_sample(u, y, h0_re, h0_im, s5p, l)
        g_sub = vec(g_subln, l)
        o = _attn_prompt(lam, q_b, k_b, v_b, g_sub, n_batch, seq, 1.0 - lambda_init)
        o = _attn_sample(page_table, lam, q_f, k_s, v_s, o, g_sub, kt_pool, v_pool, l,
                         1.0 - lambda_init)
        x = _merge(x, y, o, vec(g_mix_pre, l), w_in_b, wbs_b, wba_b, wo_b, vec(g_mix_post, l), l)
        x = _ffn(x, vec(g_ffn2_pre, l), vec(g_ffn2_post, l), *ffn2, l)

        for lst, val in zip(s5_states, (sp_re, sp_im, ss_re, ss_im)):
            lst.append(val)

    sp_re, sp_im, ss_re, ss_im = [jnp.stack(s) for s in s5_states]
    k_p = kt_p.reshape(depth, n_batch, N_HEADS, 2, HEAD_DIM, seq).transpose(0, 1, 5, 2, 3, 4)
    return (x[:n_prompt].reshape(n_batch, seq, d_model),
            x[n_prompt:].reshape(n_seq, DEC_SEQ, d_model),
            k_p,
            v_p.reshape(depth, n_batch, seq, N_HEADS, V_DIM),
            sp_re.reshape(depth, n_batch, groups, n_p),
            sp_im.reshape(depth, n_batch, groups, n_p),
            k_s.reshape(depth, n_seq, DEC_SEQ, N_HEADS, 2, HEAD_DIM),
            v_s.reshape(depth, n_seq, DEC_SEQ, N_HEADS, V_DIM),
            ss_re.reshape(depth, n_seq, groups, n_p),
            ss_im.reshape(depth, n_seq, groups, n_p))
```

```python
import functools
import math

import jax
import jax.numpy as jnp
from jax import lax
from jax.experimental import pallas as pl
from jax.experimental.pallas import tpu as pltpu

F32 = jnp.float32
BF16 = jnp.bfloat16

PAGE_SIZE = 128
DEC_SEQ = 8
S5_GROUP = 16
S5_STATE = 64
N_HEADS = 4
HEAD_DIM = 64
V_DIM = 2 * HEAD_DIM
RMS_EPS = 1e-6
SUBLN_EPS = 1e-5
NEG_BIG = -1e30
Q_SCALE = HEAD_DIM ** -0.5 * math.log2(math.e)

LANES = 128
SUBLANES = 8
GROUPS_PER_SLAB = LANES // S5_GROUP
SLAB_STATES = GROUPS_PER_SLAB * S5_STATE
FF_CHUNK = 256
VMEM_LIMIT = 56 * 1024 * 1024


def _largest_tile(n, pref):
    t = pref
    while n % t:
        t //= 2
    return t


def _rms(x, g, eps):
    return x * lax.rsqrt(jnp.mean(x * x, axis=-1, keepdims=True) + eps) * g


def _cparams(*sem):
    return pltpu.CompilerParams(dimension_semantics=sem, vmem_limit_bytes=VMEM_LIMIT)


def _resident(shape):
    nd = len(shape)
    return pl.BlockSpec(shape, lambda *_: (0,) * nd, pipeline_mode=pl.Buffered(1))


def _layer_resident(stack, layer, block=None, index=None):
    block = tuple(stack.shape[1:]) if block is None else block
    index = (0,) * len(block) if index is None else index
    return pl.BlockSpec((None,) + block, lambda *_: (layer,) + index, pipeline_mode=pl.Buffered(1))


def _ffn_kernel(x_ref, gpre_ref, gpost_ref, wg_ref, wu_ref, wd_ref, o_ref, h_scr, acc_scr):
    x = x_ref[...]
    h_scr[...] = _rms(x, gpre_ref[...], RMS_EPS).astype(BF16)
    for c in range(wg_ref.shape[1] // FF_CHUNK):
        cols = slice(c * FF_CHUNK, (c + 1) * FF_CHUNK)
        g = jnp.dot(h_scr[...], wg_ref[:, cols], preferred_element_type=F32)
        u = jnp.dot(h_scr[...], wu_ref[:, cols], preferred_element_type=F32)
        a = (g * jax.nn.sigmoid(g) * u).astype(BF16)
        part = jnp.dot(a, wd_ref[cols, :], preferred_element_type=F32)
        if c == 0:
            acc_scr[...] = part
        else:
            acc_scr[...] += part
    o_ref[...] = x + 0.5 * _rms(acc_scr[...], gpost_ref[...], RMS_EPS)


def _ffn(x, gpre, gpost, wg, wu, wd, layer):
    t, d = x.shape
    tm = _largest_tile(t, 1024)
    row = pl.BlockSpec((tm, d), lambda i: (i, 0))
    vec = pl.BlockSpec((1, d), lambda i: (0, 0))
    return pl.pallas_call(
        _ffn_kernel,
        out_shape=jax.ShapeDtypeStruct((t, d), F32),
        grid=(t // tm,),
        in_specs=[row, vec, vec] + [_layer_resident(w, layer) for w in (wg, wu, wd)],
        out_specs=row,
        scratch_shapes=[pltpu.VMEM((tm, d), BF16), pltpu.VMEM((tm, d), F32)],
        compiler_params=_cparams("arbitrary"),
        name="ffn",
    )(x, gpre, gpost, wg, wu, wd)


def _inproj_kernel(x_ref, g_ref, w_ref, *rest, d_s5, d_qk, d_att, n_prompt_tiles):
    u_ref, qf_ref, qb_ref, kb_ref, vb_ref, ktp_ref, ks_ref, vp_ref, vs_ref = rest[-9:]
    i = pl.program_id(0)
    h = _rms(x_ref[...], g_ref[...], RMS_EPS).astype(BF16)

    def proj(lo, width):
        return jnp.dot(h, w_ref[:, lo:lo + width], preferred_element_type=F32)

    u = proj(0, d_s5)
    for j in range(d_s5 // LANES):
        u_ref[j] = u[:, j * LANES:(j + 1) * LANES]
    q = proj(d_s5, d_qk) * Q_SCALE
    qf_ref[...] = q
    qb_ref[...] = q.astype(BF16)
    k = proj(d_s5 + d_qk, d_qk)
    kb_ref[...] = k.astype(BF16)
    v = proj(d_s5 + 2 * d_qk, d_att)
    vb_ref[...] = v.astype(BF16)

    def store_heads(v_ref):
        for hd in range(N_HEADS):
            v_ref[pl.ds(hd, v.shape[0], stride=N_HEADS), :] = v[:, hd * V_DIM:(hd + 1) * V_DIM]

    @pl.when(i < n_prompt_tiles)
    def _():
        ktp_ref[...] = k.T
        store_heads(vp_ref)

    @pl.when(i >= n_prompt_tiles)
    def _():
        ks_ref[...] = k
        store_heads(vs_ref)


def _inproj(x, g, w, stacks, layer, depth, n_batch, seq, d_s5, d_qk, d_att):
    t, d = x.shape
    tm = _largest_tile(math.gcd(seq, t - n_batch * seq), 512)
    np_tiles = n_batch * seq // tm
    per_batch = seq // tm
    rows_s = t - n_batch * seq
    d_proj = d_s5 + 2 * d_qk + d_att

    def rows(width):
        return pl.BlockSpec((tm, width), lambda i: (i, 0))

    def p_tile(i):
        return jnp.minimum(i, np_tiles - 1)

    def s_tile(i):
        return jnp.maximum(i - np_tiles, 0)

    n_slab = d_s5 // LANES
    any_spec = pl.BlockSpec(memory_space=pl.ANY)
    n_in = 3
    return pl.pallas_call(
        functools.partial(_inproj_kernel, d_s5=d_s5, d_qk=d_qk, d_att=d_att,
                          n_prompt_tiles=np_tiles),
        out_shape=(jax.ShapeDtypeStruct((n_slab, t, LANES), F32),
                   jax.ShapeDtypeStruct((t, d_qk), F32),
                   jax.ShapeDtypeStruct((t, d_qk), BF16),
                   jax.ShapeDtypeStruct((t, d_qk), BF16),
                   jax.ShapeDtypeStruct((t, d_att), BF16),
                   jax.ShapeDtypeStruct((depth, n_batch, d_qk, seq), F32),
                   jax.ShapeDtypeStruct((depth, rows_s, d_qk), F32),
                   jax.ShapeDtypeStruct((depth, n_batch * seq * N_HEADS, V_DIM), F32),
                   jax.ShapeDtypeStruct((depth, rows_s * N_HEADS, V_DIM), F32)),
        grid=(t // tm,),
        in_specs=[rows(d), pl.BlockSpec((1, d), lambda i: (0, 0)),
                  _layer_resident(w, layer, block=(d, d_proj))] + [any_spec] * len(stacks),
        out_specs=(pl.BlockSpec((n_slab, tm, LANES), lambda i: (0, i, 0)),
                   rows(d_qk), rows(d_qk), rows(d_qk), rows(d_att),
                   pl.BlockSpec((None, None, d_qk, tm),
                                lambda i: (layer, p_tile(i) // per_batch, 0, p_tile(i) % per_batch)),
                   pl.BlockSpec((None, tm, d_qk), lambda i: (layer, s_tile(i), 0)),
                   pl.BlockSpec((None, tm * N_HEADS, V_DIM), lambda i: (layer, p_tile(i), 0)),
                   pl.BlockSpec((None, tm * N_HEADS, V_DIM), lambda i: (layer, s_tile(i), 0))),
        input_output_aliases={n_in + s: 5 + s for s in range(len(stacks))},
        compiler_params=_cparams("arbitrary"),
        name="in_proj",
    )(x, g, w, *stacks)


def _s5_readout(st_scr, up_scr, cbd_ref, d_ref, wglu_ref, n_slab):
    ys = []
    for j in range(n_slab):
        hj = st_scr[:, 2 * SLAB_STATES * j:2 * SLAB_STATES * (j + 1)].astype(BF16)
        ys.append(jnp.dot(hj, cbd_ref[j], preferred_element_type=F32))
    y = jnp.concatenate(ys, axis=-1) + d_ref[...] * up_scr[...]
    z = jax.nn.gelu(y).astype(BF16)
    zg = jnp.dot(z, wglu_ref[...], preferred_element_type=F32)
    half = zg.shape[1] // 2
    return zg[:, :half] * jax.nn.sigmoid(zg[:, half:])


def _s5_input(st_scr, up_scr, bbd_ref, n_slab):
    for j in range(n_slab):
        uj = up_scr[:, j * LANES:(j + 1) * LANES].astype(BF16)
        st_scr[:, 2 * SLAB_STATES * j:2 * SLAB_STATES * (j + 1)] = jnp.dot(
            uj, bbd_ref[j], preferred_element_type=F32)


def _s5_prompt_kernel(u_ref, bbd_ref, cbd_ref, ab_ref, aps_ref, pw_ref, d_ref, wglu_ref,
                      y_ref, sre_ref, sim_ref, up_scr, st_scr, carry_scr, *, tc):
    c = pl.program_id(1)
    n_slab = u_ref.shape[0]
    seg = tc // SUBLANES
    w = SLAB_STATES

    @pl.when(c == 0)
    def _():
        carry_scr[...] = jnp.zeros_like(carry_scr)

    for j in range(n_slab):
        for k in range(seg):
            up_scr[k * SUBLANES:(k + 1) * SUBLANES, j * LANES:(j + 1) * LANES] = (
                u_ref[j, pl.ds(k, SUBLANES, stride=seg), :])
    _s5_input(st_scr, up_scr, bbd_ref, n_slab)

    row_id = lax.broadcasted_iota(jnp.int32, (SUBLANES, w), 0)
    for j in range(n_slab):
        re = slice(2 * w * j, 2 * w * j + w)
        im = slice(2 * w * j + w, 2 * w * (j + 1))
        a_re = ab_ref[j, 0]
        a_im = ab_ref[j, 1]

        def scan_step(k, h, re=re, im=im, a_re=a_re, a_im=a_im):
            h_re, h_im = h
            r = pl.ds(pl.multiple_of(k * SUBLANES, SUBLANES), SUBLANES)
            n_re = a_re * h_re - a_im * h_im + st_scr[r, re]
            n_im = a_re * h_im + a_im * h_re + st_scr[r, im]
            st_scr[r, re] = n_re
            st_scr[r, im] = n_im
            return n_re, n_im

        zero = jnp.zeros((SUBLANES, w), F32)
        e_re, e_im = lax.fori_loop(0, seg, scan_step, (zero, zero), unroll=True)

        s_re = aps_ref[j, 0][0:1]
        s_im = aps_ref[j, 1][0:1]
        cur_re = carry_scr[0:1, re]
        cur_im = carry_scr[0:1, im]
        c_re = jnp.broadcast_to(cur_re, (SUBLANES, w))
        c_im = jnp.broadcast_to(cur_im, (SUBLANES, w))
        for i in range(1, SUBLANES + 1):
            nxt_re = s_re * cur_re - s_im * cur_im + e_re[i - 1:i]
            nxt_im = s_re * cur_im + s_im * cur_re + e_im[i - 1:i]
            cur_re, cur_im = nxt_re, nxt_im
            if i < SUBLANES:
                c_re = jnp.where(row_id == i, cur_re, c_re)
                c_im = jnp.where(row_id == i, cur_im, c_im)
        carry_scr[0:1, re] = cur_re
        carry_scr[0:1, im] = cur_im

        def fix_step(k, carry, re=re, im=im, c_re=c_re, c_im=c_im, j=j):
            r = pl.ds(pl.multiple_of(k * SUBLANES, SUBLANES), SUBLANES)
            p_re = pw_ref[j, 0, k]
            p_im = pw_ref[j, 1, k]
            st_scr[r, re] += p_re * c_re - p_im * c_im
            st_scr[r, im] += p_re * c_im + p_im * c_re
            return carry

        lax.fori_loop(0, seg, fix_step, 0, unroll=True)

    up_scr[...] = _s5_readout(st_scr, up_scr, cbd_ref, d_ref, wglu_ref, n_slab)
    for j in range(n_slab):
        for k in range(seg):
            y_ref[j, pl.ds(k, SUBLANES, stride=seg), :] = (
                up_scr[k * SUBLANES:(k + 1) * SUBLANES, j * LANES:(j + 1) * LANES])

    @pl.when(c == pl.num_programs(1) - 1)
    def _():
        for j in range(n_slab):
            sre_ref[0, :, j * w:(j + 1) * w] = carry_scr[0:1, 2 * w * j:2 * w * j + w]
            sim_ref[0, :, j * w:(j + 1) * w] = carry_scr[0:1, 2 * w * j + w:2 * w * (j + 1)]


def _s5_prompt(u_slab, n_batch, seq, p, layer):
    n_slab = u_slab.shape[0]
    d_s5 = n_slab * LANES
    tc = p["tc"]
    n_chunks = seq // tc
    n_state = n_slab * SLAB_STATES
    u_spec = pl.BlockSpec((n_slab, tc, LANES), lambda b, c: (0, b * n_chunks + c, 0))
    st_spec = pl.BlockSpec((1, 1, n_state), lambda b, c: (b, 0, 0))
    return pl.pallas_call(
        functools.partial(_s5_prompt_kernel, tc=tc),
        out_shape=(jax.ShapeDtypeStruct(u_slab.shape, F32),
                   jax.ShapeDtypeStruct((n_batch, 1, n_state), F32),
                   jax.ShapeDtypeStruct((n_batch, 1, n_state), F32)),
        grid=(n_batch, n_chunks),
        in_specs=[u_spec] + [_layer_resident(p[k], layer)
                             for k in ("bbd", "cbd", "ab", "aps", "pw", "d", "wglu")],
        out_specs=(u_spec, st_spec, st_spec),
        scratch_shapes=[pltpu.VMEM((tc, d_s5), F32),
                        pltpu.VMEM((tc, 2 * n_state), F32),
                        pltpu.VMEM((SUBLANES, 2 * n_state), F32)],
        compiler_params=_cparams("arbitrary", "arbitrary"),
        name="s5_prompt",
    )(u_slab, p["bbd"], p["cbd"], p["ab"], p["aps"], p["pw"], p["d"], p["wglu"])


def _s5_sample_kernel(u_ref, y_alias_ref, h0re_ref, h0im_ref, bbd_ref, cbd_ref, ab_ref, d_ref,
                      wglu_ref, y_ref, sre_ref, sim_ref, up_scr, st_scr):
    del y_alias_ref
    n_slab = u_ref.shape[0]
    n_seq = h0re_ref.shape[0]
    w = SLAB_STATES
    for j in range(n_slab):
        for t in range(DEC_SEQ):
            up_scr[t * n_seq:(t + 1) * n_seq, j * LANES:(j + 1) * LANES] = (
                u_ref[j, pl.ds(t, n_seq, stride=DEC_SEQ), :])
    _s5_input(st_scr, up_scr, bbd_ref, n_slab)

    for j in range(n_slab):
        re = slice(2 * w * j, 2 * w * j + w)
        im = slice(2 * w * j + w, 2 * w * (j + 1))
        a_re = ab_ref[j, 0]
        a_im = ab_ref[j, 1]

        def seq_group(g, carry, re=re, im=im, a_re=a_re, a_im=a_im, j=j):
            n0 = pl.multiple_of(g * SUBLANES, SUBLANES)
            h_re = h0re_ref[pl.ds(n0, SUBLANES), j * w:(j + 1) * w]
            h_im = h0im_ref[pl.ds(n0, SUBLANES), j * w:(j + 1) * w]
            for t in range(DEC_SEQ):
                r = pl.ds(pl.multiple_of(t * n_seq + n0, SUBLANES), SUBLANES)
                n_re = a_re * h_re - a_im * h_im + st_scr[r, re]
                n_im = a_re * h_im + a_im * h_re + st_scr[r, im]
                st_scr[r, re] = n_re
                st_scr[r, im] = n_im
                h_re, h_im = n_re, n_im
            sre_ref[pl.ds(n0, SUBLANES), j * w:(j + 1) * w] = h_re
            sim_ref[pl.ds(n0, SUBLANES), j * w:(j + 1) * w] = h_im
            return carry

        lax.fori_loop(0, n_seq // SUBLANES, seq_group, 0)

    up_scr[...] = _s5_readout(st_scr, up_scr, cbd_ref, d_ref, wglu_ref, n_slab)
    for j in range(n_slab):
        for t in range(DEC_SEQ):
            y_ref[j, pl.ds(t, n_seq, stride=DEC_SEQ), :] = (
                up_scr[t * n_seq:(t + 1) * n_seq, j * LANES:(j + 1) * LANES])


def _s5_sample(u_slab, y_slab, h0_re, h0_im, p, layer):
    n_slab, t, _ = u_slab.shape
    n_seq, n_state = h0_re.shape[1:]
    rows = n_seq * DEC_SEQ
    tail = pl.BlockSpec((n_slab, rows, LANES), lambda i: (0, (t - rows) // rows, 0))
    params = (h0_re, h0_im, p["bbd"], p["cbd"], p["ab"], p["d"], p["wglu"])
    return pl.pallas_call(
        _s5_sample_kernel,
        out_shape=(jax.ShapeDtypeStruct(y_slab.shape, F32),
                   jax.ShapeDtypeStruct((n_seq, n_state), F32),
                   jax.ShapeDtypeStruct((n_seq, n_state), F32)),
        grid=(1,),
        in_specs=[tail, pl.BlockSpec(memory_space=pl.ANY)]
        + [_layer_resident(a, layer) for a in params],
        out_specs=(tail, _resident((n_seq, n_state)), _resident((n_seq, n_state))),
        scratch_shapes=[pltpu.VMEM((rows, n_slab * LANES), F32),
                        pltpu.VMEM((rows, 2 * n_state), F32)],
        input_output_aliases={1: 0},
        compiler_params=_cparams("arbitrary"),
        name="s5_sample",
    )(u_slab, y_slab, *params)


def _subln(o, g_ref, post_scale):
    return _rms(o, g_ref[...], SUBLN_EPS) * post_scale


def _prompt_attention(lam, qi, q_ref, k_ref, v_ref, g_ref, o_ref, q2_scr, m_scr, l_scr, acc_scr,
                      blk, post_scale):
    q = q_ref[...]
    lane = lax.broadcasted_iota(jnp.int32, q.shape, 1)
    zero = jnp.zeros_like(q)
    q2_scr[0:blk] = jnp.where(lane < HEAD_DIM, q, zero)
    q2_scr[blk:2 * blk] = jnp.where(lane >= HEAD_DIM, q, zero)
    m_scr[...] = jnp.full_like(m_scr, NEG_BIG)
    l_scr[...] = jnp.zeros_like(l_scr)
    acc_scr[...] = jnp.zeros_like(acc_scr)

    n_rep = blk // LANES

    def block(kj, masked):
        r = pl.ds(pl.multiple_of(kj * blk, blk), blk)
        s = lax.dot_general(q2_scr[...], k_ref[r, :], (((1,), (1,)), ((), ())),
                            preferred_element_type=F32)
        if masked:
            qpos = lax.broadcasted_iota(jnp.int32, s.shape, 0) % blk
            kpos = lax.broadcasted_iota(jnp.int32, s.shape, 1)
            s = jnp.where(kpos <= qpos, s, NEG_BIG)
        m_old = m_scr[...]
        m_new = jnp.maximum(m_old, jnp.max(s, axis=-1, keepdims=True))
        alpha = jnp.exp2(m_old - m_new)
        p = jnp.exp2(s - jnp.concatenate([m_new] * n_rep, axis=-1))
        l_scr[...] = alpha * l_scr[...] + jnp.sum(p, axis=-1, keepdims=True)
        acc_scr[...] = alpha * acc_scr[...] + jnp.dot(p.astype(BF16), v_ref[r, :],
                                                      preferred_element_type=F32)
        m_scr[...] = m_new

    def full_block(kj, carry):
        block(kj, False)
        return carry

    lax.fori_loop(0, qi, full_block, 0)
    block(qi, True)

    o1 = acc_scr[0:blk] / l_scr[0:blk]
    o2 = acc_scr[blk:2 * blk] / l_scr[blk:2 * blk]
    o_ref[...] = _subln(o1 - lam * o2, g_ref, post_scale)


def _sample_attention(lam, q_ref, kn_ref, vn_ref, g_ref, kt_pages, v_pages, o_ref, kt_scr, v_scr,
                      post_scale):
    n_row = N_HEADS * 2 * DEC_SEQ

    for j in range(len(kt_pages)):
        cols = slice(j * PAGE_SIZE, (j + 1) * PAGE_SIZE)
        kt_scr[:, cols] = kt_pages[j][...].astype(BF16)
        for h in range(N_HEADS):
            v_scr[h, cols, :] = v_pages[j][pl.ds(h, PAGE_SIZE, stride=N_HEADS), :].astype(BF16)
    def pad_rows(x):
        pad = jnp.zeros((PAGE_SIZE - DEC_SEQ, x.shape[1]), F32)
        return jnp.concatenate([x, pad], axis=0).astype(BF16)

    k_new = pad_rows(kn_ref[...])

    qt = jnp.concatenate([q_ref[...]] * (n_row // DEC_SEQ), axis=0)
    rid = lax.broadcasted_iota(jnp.int32, qt.shape, 0)
    blk_id = lax.broadcasted_iota(jnp.int32, qt.shape, 1) // HEAD_DIM
    qm = jnp.where(blk_id == rid // DEC_SEQ, qt, 0.0).astype(BF16)

    s = jnp.dot(qm, kt_scr[...], preferred_element_type=F32)
    s_new = lax.dot_general(qm, k_new, (((1,), (1,)), ((), ())), preferred_element_type=F32)
    t_new = lax.broadcasted_iota(jnp.int32, s_new.shape, 1)
    q_idx = lax.broadcasted_iota(jnp.int32, s_new.shape, 0) % DEC_SEQ
    s_new = jnp.where(t_new <= q_idx, s_new, NEG_BIG)

    m = jnp.maximum(jnp.max(s, axis=-1, keepdims=True), jnp.max(s_new, axis=-1, keepdims=True))
    p = jnp.exp2(s - m)
    p_new = jnp.exp2(s_new - m)
    l = jnp.sum(p, axis=-1, keepdims=True) + jnp.sum(p_new, axis=-1, keepdims=True)
    pb = p.astype(BF16)
    pb_new = p_new.astype(BF16)
    for h in range(N_HEADS):
        rows = slice(2 * DEC_SEQ * h, 2 * DEC_SEQ * (h + 1))
        v_new = pad_rows(vn_ref[pl.ds(h, DEC_SEQ, stride=N_HEADS), :])
        o_h = (jnp.dot(pb[rows], v_scr[h], preferred_element_type=F32)
               + jnp.dot(pb_new[rows], v_new, preferred_element_type=F32)) / l[rows]
        o_ref[:, h * V_DIM:(h + 1) * V_DIM] = _subln(
            o_h[0:DEC_SEQ] - lam * o_h[DEC_SEQ:2 * DEC_SEQ], g_ref, post_scale)


def _attn_kernel(pt_ref, lam_ref, q_ref, k_ref, v_ref, g_ref, qs_ref, kn_ref, vn_ref, *rest,
                 blk, n_pages, n_seq, post_scale):
    del pt_ref
    kt_pages = rest[:n_pages]
    v_pages = rest[n_pages:2 * n_pages]
    o_ref, os_ref = rest[2 * n_pages:2 * n_pages + 2]
    q2_scr, m_scr, l_scr, acc_scr, kt_scr, v_scr = rest[2 * n_pages + 2:]
    lam = lam_ref[0, 0]
    qi = pl.program_id(2)
    step = (pl.program_id(0) * pl.num_programs(1) + pl.program_id(1)) * pl.num_programs(2) + qi
    _prompt_attention(lam, qi, q_ref, k_ref, v_ref, g_ref, o_ref, q2_scr, m_scr, l_scr, acc_scr,
                      blk, post_scale)

    @pl.when(step < n_seq)
    def _():
        _sample_attention(lam, qs_ref, kn_ref, vn_ref, g_ref, kt_pages, v_pages, os_ref,
                          kt_scr, v_scr, post_scale)


def _attention(page_table, lam, q_b, k_b, v_b, q_f, k_s, v_s, g_subln, kt_pool, v_pool, layer,
               n_batch, seq, post_scale):
    n_seq, n_pages = page_table.shape
    t, width = q_f.shape
    past = n_pages * PAGE_SIZE
    blk = _largest_tile(seq, 512)
    nq = seq // blk
    assert n_seq <= n_batch * N_HEADS * nq, "one sample sequence per grid step"
    first = t // DEC_SEQ - n_seq

    def sample(b, h, i):
        return jnp.minimum((b * N_HEADS + h) * nq + i, n_seq - 1)

    q_spec = pl.BlockSpec((blk, V_DIM), lambda b, h, i, pt: (b * nq + i, h))
    kv_spec = pl.BlockSpec((seq, V_DIM), lambda b, h, i, pt: (b, h))
    tok = pl.BlockSpec((DEC_SEQ, width), lambda b, h, i, pt: (first + sample(b, h, i), 0))
    new_k = pl.BlockSpec((None, DEC_SEQ, width), lambda b, h, i, pt: (layer, sample(b, h, i), 0))
    new_v = pl.BlockSpec((None, DEC_SEQ * N_HEADS, V_DIM),
                         lambda b, h, i, pt: (layer, sample(b, h, i), 0))

    def page_spec(shape, j):
        return pl.BlockSpec((None, None) + shape,
                            lambda b, h, i, pt, j=j: (layer, pt[sample(b, h, i), j], 0, 0))

    kt_specs = [page_spec(kt_pool.shape[2:], j) for j in range(n_pages)]
    v_specs = [page_spec(v_pool.shape[2:], j) for j in range(n_pages)]
    grid_spec = pltpu.PrefetchScalarGridSpec(
        num_scalar_prefetch=1,
        grid=(n_batch, N_HEADS, nq),
        in_specs=[pl.BlockSpec(memory_space=pltpu.SMEM), q_spec, kv_spec, kv_spec,
                  pl.BlockSpec((1, V_DIM), lambda b, h, i, pt: (0, 0)),
                  tok, new_k, new_v] + kt_specs + v_specs,
        out_specs=(q_spec,
                   pl.BlockSpec((DEC_SEQ, width), lambda b, h, i, pt: (sample(b, h, i), 0))),
        scratch_shapes=[pltpu.VMEM((2 * blk, V_DIM), BF16),
                        pltpu.VMEM((2 * blk, LANES), F32),
                        pltpu.VMEM((2 * blk, LANES), F32),
                        pltpu.VMEM((2 * blk, V_DIM), F32),
                        pltpu.VMEM((width, past), BF16),
                        pltpu.VMEM((N_HEADS, past, V_DIM), BF16)],
    )
    return pl.pallas_call(
        functools.partial(_attn_kernel, blk=blk, n_pages=n_pages, n_seq=n_seq,
                          post_scale=post_scale),
        out_shape=(jax.ShapeDtypeStruct((n_batch * seq, N_HEADS * V_DIM), F32),
                   jax.ShapeDtypeStruct((n_seq * DEC_SEQ, width), F32)),
        grid_spec=grid_spec,
        compiler_params=_cparams("arbitrary", "arbitrary", "arbitrary"),
        name="attention",
    )(page_table, lam, q_b, k_b, v_b, g_subln, q_f, k_s, v_s,
      *([kt_pool] * n_pages), *([v_pool] * n_pages))


def _merge_kernel(x_ref, ys_ref, yap_ref, yas_ref, gpre_ref, wgate_ref, wbs_ref, wba_ref, wo_ref,
                  g_ref, o_ref, *, n_prompt_tiles):
    d = x_ref.shape[1]
    ys = jnp.concatenate([ys_ref[j] for j in range(ys_ref.shape[0])], axis=-1).astype(BF16)
    ya = jnp.where(pl.program_id(0) < n_prompt_tiles, yap_ref[...], yas_ref[...]).astype(BF16)
    h = _rms(x_ref[...], gpre_ref[...], RMS_EPS).astype(BF16)
    gate = jnp.dot(h, wgate_ref[...], preferred_element_type=F32)
    merged = (jax.nn.sigmoid(gate[:, :d]) * jnp.dot(ys, wbs_ref[...], preferred_element_type=F32)
              + jax.nn.sigmoid(gate[:, d:]) * jnp.dot(ya, wba_ref[...], preferred_element_type=F32))
    out = jnp.dot(merged.astype(BF16), wo_ref[...], preferred_element_type=F32)
    o_ref[...] = x_ref[...] + _rms(out, g_ref[...], RMS_EPS)


def _merge(x, y_s5, y_att_p, y_att_s, g_pre, w_in, wbs, wba, wo, g_post, layer):
    t, d = x.shape
    tm = _largest_tile(math.gcd(y_att_p.shape[0], y_att_s.shape[0]), 512)
    np_tiles = y_att_p.shape[0] // tm
    n_slab = y_s5.shape[0]
    width = y_att_p.shape[1]

    def rows(width):
        return pl.BlockSpec((tm, width), lambda i: (i, 0))

    vec = pl.BlockSpec((1, d), lambda i: (0, 0))
    return pl.pallas_call(
        functools.partial(_merge_kernel, n_prompt_tiles=np_tiles),
        out_shape=jax.ShapeDtypeStruct((t, d), F32),
        grid=(t // tm,),
        in_specs=[rows(d), pl.BlockSpec((n_slab, tm, LANES), lambda i: (0, i, 0)),
                  pl.BlockSpec((tm, width), lambda i: (jnp.minimum(i, np_tiles - 1), 0)),
                  pl.BlockSpec((tm, width), lambda i: (jnp.maximum(i - np_tiles, 0), 0)), vec,
                  _layer_resident(w_in, layer, block=(d, 2 * d), index=(0, w_in.shape[2] // (2 * d) - 1)),
                  _layer_resident(wbs, layer), _layer_resident(wba, layer),
                  _layer_resident(wo, layer), vec],
        out_specs=rows(d),
        compiler_params=_cparams("arbitrary"),
        name="merge",
    )(x, y_s5, y_att_p, y_att_s, g_pre, w_in, wbs, wba, wo, g_post)


def _block_diag(m):
    eye = jnp.eye(GROUPS_PER_SLAB, dtype=m.dtype)
    depth, n_slab, g, r, c = m.shape
    return jnp.einsum("ljgrc,gh->ljgrhc", m, eye).reshape(depth, n_slab, g * r, g * c)


def _s5_params(a_re, a_im, log_dt, b_re, b_im, c_re, c_im, d, w_glu, tc):
    depth, groups, n_p = a_re.shape
    n_slab = groups // GROUPS_PER_SLAB
    seg = tc // SUBLANES
    dt = jnp.exp(log_dt)[:, :, None]
    mag = jnp.exp(dt * a_re)
    abar_re, abar_im = mag * jnp.cos(dt * a_im), mag * jnp.sin(dt * a_im)
    den = a_re * a_re + a_im * a_im
    inv_re, inv_im = a_re / den, -a_im / den
    fac_re = (abar_re - 1.0) * inv_re - abar_im * inv_im
    fac_im = (abar_re - 1.0) * inv_im + abar_im * inv_re
    fb_re = fac_re[..., None] * b_re - fac_im[..., None] * b_im
    fb_im = fac_re[..., None] * b_im + fac_im[..., None] * b_re

    def slab(x):
        return x.reshape((depth, n_slab, GROUPS_PER_SLAB) + x.shape[2:])

    bbd = jnp.concatenate([_block_diag(slab(fb_re).swapaxes(-1, -2)),
                           _block_diag(slab(fb_im).swapaxes(-1, -2))], axis=-1).astype(BF16)
    cbd = jnp.concatenate([_block_diag(slab(c_re).swapaxes(-1, -2)),
                           _block_diag(slab(-c_im).swapaxes(-1, -2))], axis=-2).astype(BF16)

    def lanes(x):
        return x.reshape(depth, n_slab, SLAB_STATES)

    k = jnp.arange(1, seg + 1, dtype=F32)[None, :, None, None]
    pw_mag = jnp.exp(k * (dt * a_re)[:, None])
    pw_arg = k * (dt * a_im)[:, None]
    pw_re, pw_im = pw_mag * jnp.cos(pw_arg), pw_mag * jnp.sin(pw_arg)

    def rows8(x):
        return jnp.broadcast_to(x[..., None, :], x.shape[:-1] + (SUBLANES, x.shape[-1]))

    def seg_lanes(x):
        return x.reshape(depth, seg, n_slab, SLAB_STATES).swapaxes(1, 2)

    ab = rows8(jnp.stack([lanes(abar_re), lanes(abar_im)], axis=2))
    aps = rows8(jnp.stack([lanes(pw_re[:, -1]), lanes(pw_im[:, -1])], axis=2))
    pw = rows8(jnp.stack([seg_lanes(pw_re), seg_lanes(pw_im)], axis=2))
    return dict(bbd=bbd, cbd=cbd, ab=ab, aps=aps, pw=pw,
                d=d[:, None, :], wglu=w_glu.astype(BF16))


def kernel(x_prompt, x_sample, cache_k, cache_v, state_s5_re, state_s5_im, page_table, g_ffn1_pre, g_ffn1_post, w_ffn1_gate, w_ffn1_up, w_ffn1_down, g_mix_pre, g_mix_post, w_in, s5_a_re, s5_a_im, s5_log_dt, s5_b_re, s5_b_im, s5_c_re, s5_c_im, s5_d, w_glu, lambda_q1, lambda_k1, lambda_q2, lambda_k2, g_subln, w_branch_s5, w_branch_att, w_out, g_ffn2_pre, g_ffn2_post, w_ffn2_gate, w_ffn2_up, w_ffn2_down):
    n_batch, seq, d_model = x_prompt.shape
    n_seq = x_sample.shape[0]
    depth = w_in.shape[0]
    n_prompt = n_batch * seq
    groups, n_p = s5_a_re.shape[1:]
    d_s5 = groups * S5_GROUP
    d_qk = N_HEADS * 2 * HEAD_DIM
    d_att = N_HEADS * V_DIM
    n_state = groups * n_p

    x = jnp.concatenate([x_prompt.reshape(n_prompt, d_model),
                         x_sample.reshape(n_seq * DEC_SEQ, d_model)], axis=0)

    tc = _largest_tile(seq, 512)
    ffn1 = [w.astype(BF16) for w in (w_ffn1_gate, w_ffn1_up, w_ffn1_down)]
    ffn2 = [w.astype(BF16) for w in (w_ffn2_gate, w_ffn2_up, w_ffn2_down)]
    w_in_b = w_in.astype(BF16)
    wbs_b, wba_b, wo_b = w_branch_s5.astype(BF16), w_branch_att.astype(BF16), w_out.astype(BF16)
    s5p = _s5_params(s5_a_re, s5_a_im, s5_log_dt, s5_b_re, s5_b_im, s5_c_re, s5_c_im,
                     s5_d, w_glu, tc)
    s5p["tc"] = tc
    lam_dyn = (jnp.exp(jnp.sum(lambda_q1 * lambda_k1, axis=-1))
               - jnp.exp(jnp.sum(lambda_q2 * lambda_k2, axis=-1)))
    n_phys = cache_k.shape[1]
    kt_pool = cache_k.transpose(0, 1, 3, 4, 5, 2).reshape(depth, n_phys, d_qk, PAGE_SIZE)
    v_pool = cache_v.reshape(depth, n_phys, PAGE_SIZE * N_HEADS, V_DIM)
    h0_re = state_s5_re.reshape(depth, n_seq, n_state)
    h0_im = state_s5_im.reshape(depth, n_seq, n_state)
    assert w_in.shape[2] == d_s5 + 2 * d_qk + d_att + 2 * d_model == 2 * (2 * d_model)

    def vec(g, l):
        return g[l][None, :]

    stacks = ()
    s5_states = [[] for _ in range(4)]
    for l in range(depth):
        lambda_init = 0.8 - 0.6 * math.exp(-0.3 * l)
        lam = (lam_dyn[l] + lambda_init).reshape(1, 1)

        x = _ffn(x, vec(g_ffn1_pre, l), vec(g_ffn1_post, l), *ffn1, l)
        u, q_f, q_b, k_b, v_b, *stacks = _inproj(
            x, vec(g_mix_pre, l), w_in_b, stacks, l, depth, n_batch, seq, d_s5, d_qk, d_att)
        kt_p, k_s, v_p, v_s = stacks

        y, sp_re, sp_im = _s5_prompt(u, n_batch, seq, s5p, l)
        y, ss_re, ss_im = _s5_sample(u, y, h0_re, h0_im, s5p, l)
        g_sub = vec(g_subln, l)
        o_p, o_s = _attention(page_table, lam, q_b, k_b, v_b, q_f, k_s, v_s, g_sub, kt_pool, v_pool,
                              l, n_batch, seq, 1.0 - lambda_init)
        x = _merge(x, y, o_p, o_s, vec(g_mix_pre, l), w_in_b, wbs_b, wba_b, wo_b,
                   vec(g_mix_post, l), l)
        x = _ffn(x, vec(g_ffn2_pre, l), vec(g_ffn2_post, l), *ffn2, l)

        for lst, val in zip(s5_states, (sp_re, sp_im, ss_re, ss_im)):
            lst.append(val)

    sp_re, sp_im, ss_re, ss_im = [jnp.stack(s) for s in s5_states]
    k_p = kt_p.reshape(depth, n_batch, N_HEADS, 2, HEAD_DIM, seq).transpose(0, 1, 5, 2, 3, 4)
    return (x[:n_prompt].reshape(n_batch, seq, d_model),
            x[n_prompt:].reshape(n_seq, DEC_SEQ, d_model),
            k_p,
            v_p.reshape(depth, n_batch, seq, N_HEADS, V_DIM),
            sp_re.reshape(depth, n_batch, groups, n_p),
            sp_im.reshape(depth, n_batch, groups, n_p),
            k_s.reshape(depth, n_seq, DEC_SEQ, N_HEADS, 2, HEAD_DIM),
            v_s.reshape(depth, n_seq, DEC_SEQ, N_HEADS, V_DIM),
            ss_re.reshape(depth, n_seq, groups, n_p),
            ss_im.reshape(depth, n_seq, groups, n_p))
```

```python
import functools
import math

import jax
import jax.numpy as jnp
from jax import lax
from jax.experimental import pallas as pl
from jax.experimental.pallas import tpu as pltpu

F32 = jnp.float32
BF16 = jnp.bfloat16

PAGE_SIZE = 128
DEC_SEQ = 8
S5_GROUP = 16
S5_STATE = 64
N_HEADS = 4
HEAD_DIM = 64
V_DIM = 2 * HEAD_DIM
RMS_EPS = 1e-6
SUBLN_EPS = 1e-5
NEG_BIG = -1e30
Q_SCALE = HEAD_DIM ** -0.5 * math.log2(math.e)

LANES = 128
SUBLANES = 8
GROUPS_PER_SLAB = LANES // S5_GROUP
SLAB_STATES = GROUPS_PER_SLAB * S5_STATE
FF_CHUNK = 256
VMEM_LIMIT = 56 * 1024 * 1024


def _largest_tile(n, pref):
    t = pref
    while n % t:
        t //= 2
    return t


def _rms(x, g, eps):
    return x * lax.rsqrt(jnp.mean(x * x, axis=-1, keepdims=True) + eps) * g


def _cparams(*sem):
    return pltpu.CompilerParams(dimension_semantics=sem, vmem_limit_bytes=VMEM_LIMIT)


def _resident(shape):
    nd = len(shape)
    return pl.BlockSpec(shape, lambda *_: (0,) * nd, pipeline_mode=pl.Buffered(1))


def _layer_resident(stack, layer, block=None, index=None):
    block = tuple(stack.shape[1:]) if block is None else block
    index = (0,) * len(block) if index is None else index
    return pl.BlockSpec((None,) + block, lambda *_: (layer,) + index, pipeline_mode=pl.Buffered(1))


def _ffn_kernel(*refs, n_x, n_out, n_prompt_tiles):
    x_refs = refs[:n_x]
    gpre_ref, gpost_ref, wg_ref, wu_ref, wd_ref = refs[n_x:n_x + 5]
    o_refs = refs[n_x + 5:n_x + 5 + n_out]
    h_scr, acc_scr = refs[n_x + 5 + n_out:]
    is_prompt = pl.program_id(0) < n_prompt_tiles
    x = x_refs[0][...] if n_x == 1 else jnp.where(is_prompt, x_refs[0][...], x_refs[1][...])
    h_scr[...] = _rms(x, gpre_ref[...], RMS_EPS).astype(BF16)
    for c in range(wg_ref.shape[1] // FF_CHUNK):
        cols = slice(c * FF_CHUNK, (c + 1) * FF_CHUNK)
        g = jnp.dot(h_scr[...], wg_ref[:, cols], preferred_element_type=F32)
        u = jnp.dot(h_scr[...], wu_ref[:, cols], preferred_element_type=F32)
        a = (g * jax.nn.sigmoid(g) * u).astype(BF16)
        part = jnp.dot(a, wd_ref[cols, :], preferred_element_type=F32)
        if c == 0:
            acc_scr[...] = part
        else:
            acc_scr[...] += part
    y = x + 0.5 * _rms(acc_scr[...], gpost_ref[...], RMS_EPS)
    if n_out == 1:
        o_refs[0][...] = y
    else:
        @pl.when(is_prompt)
        def _():
            o_refs[0][...] = y

        @pl.when(jnp.logical_not(is_prompt))
        def _():
            o_refs[1][...] = y


def _ffn(xs, gpre, gpost, wg, wu, wd, layer, n_prompt, split_out):
    d = xs[0].shape[1]
    t = sum(x.shape[0] for x in xs)
    tm = _largest_tile(math.gcd(n_prompt, t - n_prompt), 1024)
    np_tiles = n_prompt // tm
    row = pl.BlockSpec((tm, d), lambda i: (i, 0))
    p_row = pl.BlockSpec((tm, d), lambda i: (jnp.minimum(i, np_tiles - 1), 0))
    s_row = pl.BlockSpec((tm, d), lambda i: (jnp.maximum(i - np_tiles, 0), 0))
    vec = pl.BlockSpec((1, d), lambda i: (0, 0))
    out_shape = ((jax.ShapeDtypeStruct((n_prompt, d), F32),
                  jax.ShapeDtypeStruct((t - n_prompt, d), F32)) if split_out
                 else jax.ShapeDtypeStruct((t, d), F32))
    return pl.pallas_call(
        functools.partial(_ffn_kernel, n_x=len(xs), n_out=2 if split_out else 1,
                          n_prompt_tiles=np_tiles),
        out_shape=out_shape,
        grid=(t // tm,),
        in_specs=([row] if len(xs) == 1 else [p_row, s_row]) + [vec, vec]
        + [_layer_resident(w, layer) for w in (wg, wu, wd)],
        out_specs=(p_row, s_row) if split_out else row,
        scratch_shapes=[pltpu.VMEM((tm, d), BF16), pltpu.VMEM((tm, d), F32)],
        compiler_params=_cparams("arbitrary"),
        name="ffn",
    )(*xs, gpre, gpost, wg, wu, wd)


def _inproj_kernel(x_ref, g_ref, w_ref, *rest, d_s5, d_qk, d_att, n_prompt_tiles):
    u_ref, qf_ref, qb_ref, kb_ref, vb_ref, ktp_ref, ks_ref, vp_ref, vs_ref = rest[-9:]
    i = pl.program_id(0)
    h = _rms(x_ref[...], g_ref[...], RMS_EPS).astype(BF16)

    def proj(lo, width):
        return jnp.dot(h, w_ref[:, lo:lo + width], preferred_element_type=F32)

    u = proj(0, d_s5)
    for j in range(d_s5 // LANES):
        u_ref[j] = u[:, j * LANES:(j + 1) * LANES]
    q = proj(d_s5, d_qk) * Q_SCALE
    qf_ref[...] = q
    qb_ref[...] = q.astype(BF16)
    k = proj(d_s5 + d_qk, d_qk)
    kb_ref[...] = k.astype(BF16)
    v = proj(d_s5 + 2 * d_qk, d_att)
    vb_ref[...] = v.astype(BF16)

    def store_heads(v_ref):
        for hd in range(N_HEADS):
            v_ref[pl.ds(hd, v.shape[0], stride=N_HEADS), :] = v[:, hd * V_DIM:(hd + 1) * V_DIM]

    @pl.when(i < n_prompt_tiles)
    def _():
        ktp_ref[...] = k.T
        store_heads(vp_ref)

    @pl.when(i >= n_prompt_tiles)
    def _():
        ks_ref[...] = k
        store_heads(vs_ref)


def _inproj(x, g, w, stacks, layer, depth, n_batch, seq, d_s5, d_qk, d_att):
    t, d = x.shape
    tm = _largest_tile(math.gcd(seq, t - n_batch * seq), 512)
    np_tiles = n_batch * seq // tm
    per_batch = seq // tm
    rows_s = t - n_batch * seq
    d_proj = d_s5 + 2 * d_qk + d_att

    def rows(width):
        return pl.BlockSpec((tm, width), lambda i: (i, 0))

    def p_tile(i):
        return jnp.minimum(i, np_tiles - 1)

    def s_tile(i):
        return jnp.maximum(i - np_tiles, 0)

    n_slab = d_s5 // LANES
    any_spec = pl.BlockSpec(memory_space=pl.ANY)
    n_in = 3
    return pl.pallas_call(
        functools.partial(_inproj_kernel, d_s5=d_s5, d_qk=d_qk, d_att=d_att,
                          n_prompt_tiles=np_tiles),
        out_shape=(jax.ShapeDtypeStruct((n_slab, t, LANES), F32),
                   jax.ShapeDtypeStruct((t, d_qk), F32),
                   jax.ShapeDtypeStruct((t, d_qk), BF16),
                   jax.ShapeDtypeStruct((t, d_qk), BF16),
                   jax.ShapeDtypeStruct((t, d_att), BF16),
                   jax.ShapeDtypeStruct((depth, n_batch, d_qk, seq), F32),
                   jax.ShapeDtypeStruct((depth, rows_s, d_qk), F32),
                   jax.ShapeDtypeStruct((depth, n_batch * seq * N_HEADS, V_DIM), F32),
                   jax.ShapeDtypeStruct((depth, rows_s * N_HEADS, V_DIM), F32)),
        grid=(t // tm,),
        in_specs=[rows(d), pl.BlockSpec((1, d), lambda i: (0, 0)),
                  _layer_resident(w, layer, block=(d, d_proj))] + [any_spec] * len(stacks),
        out_specs=(pl.BlockSpec((n_slab, tm, LANES), lambda i: (0, i, 0)),
                   rows(d_qk), rows(d_qk), rows(d_qk), rows(d_att),
                   pl.BlockSpec((None, None, d_qk, tm),
                                lambda i: (layer, p_tile(i) // per_batch, 0, p_tile(i) % per_batch)),
                   pl.BlockSpec((None, tm, d_qk), lambda i: (layer, s_tile(i), 0)),
                   pl.BlockSpec((None, tm * N_HEADS, V_DIM), lambda i: (layer, p_tile(i), 0)),
                   pl.BlockSpec((None, tm * N_HEADS, V_DIM), lambda i: (layer, s_tile(i), 0))),
        input_output_aliases={n_in + s: 5 + s for s in range(len(stacks))},
        compiler_params=_cparams("arbitrary"),
        name="in_proj",
    )(x, g, w, *stacks)


def _s5_readout(st_scr, up_scr, cbd_ref, d_ref, wglu_ref, n_slab):
    ys = []
    for j in range(n_slab):
        hj = st_scr[:, 2 * SLAB_STATES * j:2 * SLAB_STATES * (j + 1)].astype(BF16)
        ys.append(jnp.dot(hj, cbd_ref[j], preferred_element_type=F32))
    y = jnp.concatenate(ys, axis=-1) + d_ref[...] * up_scr[...]
    z = jax.nn.gelu(y).astype(BF16)
    zg = jnp.dot(z, wglu_ref[...], preferred_element_type=F32)
    half = zg.shape[1] // 2
    return zg[:, :half] * jax.nn.sigmoid(zg[:, half:])


def _s5_input(st_scr, up_scr, bbd_ref, n_slab):
    for j in range(n_slab):
        uj = up_scr[:, j * LANES:(j + 1) * LANES].astype(BF16)
        st_scr[:, 2 * SLAB_STATES * j:2 * SLAB_STATES * (j + 1)] = jnp.dot(
            uj, bbd_ref[j], preferred_element_type=F32)


def _s5_prompt_kernel(u_ref, bbd_ref, cbd_ref, ab_ref, aps_ref, pw_ref, d_ref, wglu_ref,
                      y_ref, sre_ref, sim_ref, up_scr, st_scr, carry_scr, *, tc):
    c = pl.program_id(1)
    n_slab = u_ref.shape[0]
    seg = tc // SUBLANES
    w = SLAB_STATES

    @pl.when(c == 0)
    def _():
        carry_scr[...] = jnp.zeros_like(carry_scr)

    for j in range(n_slab):
        for k in range(seg):
            up_scr[k * SUBLANES:(k + 1) * SUBLANES, j * LANES:(j + 1) * LANES] = (
                u_ref[j, pl.ds(k, SUBLANES, stride=seg), :])
    _s5_input(st_scr, up_scr, bbd_ref, n_slab)

    row_id = lax.broadcasted_iota(jnp.int32, (SUBLANES, w), 0)
    for j in range(n_slab):
        re = slice(2 * w * j, 2 * w * j + w)
        im = slice(2 * w * j + w, 2 * w * (j + 1))
        a_re = ab_ref[j, 0]
        a_im = ab_ref[j, 1]

        def scan_step(k, h, re=re, im=im, a_re=a_re, a_im=a_im):
            h_re, h_im = h
            r = pl.ds(pl.multiple_of(k * SUBLANES, SUBLANES), SUBLANES)
            n_re = a_re * h_re - a_im * h_im + st_scr[r, re]
            n_im = a_re * h_im + a_im * h_re + st_scr[r, im]
            st_scr[r, re] = n_re
            st_scr[r, im] = n_im
            return n_re, n_im

        zero = jnp.zeros((SUBLANES, w), F32)
        e_re, e_im = lax.fori_loop(0, seg, scan_step, (zero, zero), unroll=True)

        s_re = aps_ref[j, 0][0:1]
        s_im = aps_ref[j, 1][0:1]
        cur_re = carry_scr[0:1, re]
        cur_im = carry_scr[0:1, im]
        c_re = jnp.broadcast_to(cur_re, (SUBLANES, w))
        c_im = jnp.broadcast_to(cur_im, (SUBLANES, w))
        for i in range(1, SUBLANES + 1):
            nxt_re = s_re * cur_re - s_im * cur_im + e_re[i - 1:i]
            nxt_im = s_re * cur_im + s_im * cur_re + e_im[i - 1:i]
            cur_re, cur_im = nxt_re, nxt_im
            if i < SUBLANES:
                c_re = jnp.where(row_id == i, cur_re, c_re)
                c_im = jnp.where(row_id == i, cur_im, c_im)
        carry_scr[0:1, re] = cur_re
        carry_scr[0:1, im] = cur_im

        def fix_step(k, carry, re=re, im=im, c_re=c_re, c_im=c_im, j=j):
            r = pl.ds(pl.multiple_of(k * SUBLANES, SUBLANES), SUBLANES)
            p_re = pw_ref[j, 0, k]
            p_im = pw_ref[j, 1, k]
            st_scr[r, re] += p_re * c_re - p_im * c_im
            st_scr[r, im] += p_re * c_im + p_im * c_re
            return carry

        lax.fori_loop(0, seg, fix_step, 0, unroll=True)

    up_scr[...] = _s5_readout(st_scr, up_scr, cbd_ref, d_ref, wglu_ref, n_slab)
    for j in range(n_slab):
        for k in range(seg):
            y_ref[j, pl.ds(k, SUBLANES, stride=seg), :] = (
                up_scr[k * SUBLANES:(k + 1) * SUBLANES, j * LANES:(j + 1) * LANES])

    @pl.when(c == pl.num_programs(1) - 1)
    def _():
        for j in range(n_slab):
            sre_ref[0, :, j * w:(j + 1) * w] = carry_scr[0:1, 2 * w * j:2 * w * j + w]
            sim_ref[0, :, j * w:(j + 1) * w] = carry_scr[0:1, 2 * w * j + w:2 * w * (j + 1)]


def _s5_prompt(u_slab, n_batch, seq, p, layer):
    n_slab = u_slab.shape[0]
    d_s5 = n_slab * LANES
    tc = p["tc"]
    n_chunks = seq // tc
    n_state = n_slab * SLAB_STATES
    u_spec = pl.BlockSpec((n_slab, tc, LANES), lambda b, c: (0, b * n_chunks + c, 0))
    st_spec = pl.BlockSpec((1, 1, n_state), lambda b, c: (b, 0, 0))
    return pl.pallas_call(
        functools.partial(_s5_prompt_kernel, tc=tc),
        out_shape=(jax.ShapeDtypeStruct(u_slab.shape, F32),
                   jax.ShapeDtypeStruct((n_batch, 1, n_state), F32),
                   jax.ShapeDtypeStruct((n_batch, 1, n_state), F32)),
        grid=(n_batch, n_chunks),
        in_specs=[u_spec] + [_layer_resident(p[k], layer)
                             for k in ("bbd", "cbd", "ab", "aps", "pw", "d", "wglu")],
        out_specs=(u_spec, st_spec, st_spec),
        scratch_shapes=[pltpu.VMEM((tc, d_s5), F32),
                        pltpu.VMEM((tc, 2 * n_state), F32),
                        pltpu.VMEM((SUBLANES, 2 * n_state), F32)],
        compiler_params=_cparams("arbitrary", "arbitrary"),
        name="s5_prompt",
    )(u_slab, p["bbd"], p["cbd"], p["ab"], p["aps"], p["pw"], p["d"], p["wglu"])


def _s5_sample_kernel(u_ref, y_alias_ref, h0re_ref, h0im_ref, bbd_ref, cbd_ref, ab_ref, d_ref,
                      wglu_ref, y_ref, sre_ref, sim_ref, up_scr, st_scr):
    del y_alias_ref
    n_slab = u_ref.shape[0]
    n_seq = h0re_ref.shape[0]
    w = SLAB_STATES
    for j in range(n_slab):
        for t in range(DEC_SEQ):
            up_scr[t * n_seq:(t + 1) * n_seq, j * LANES:(j + 1) * LANES] = (
                u_ref[j, pl.ds(t, n_seq, stride=DEC_SEQ), :])
    _s5_input(st_scr, up_scr, bbd_ref, n_slab)

    for j in range(n_slab):
        re = slice(2 * w * j, 2 * w * j + w)
        im = slice(2 * w * j + w, 2 * w * (j + 1))
        a_re = ab_ref[j, 0]
        a_im = ab_ref[j, 1]

        def seq_group(g, carry, re=re, im=im, a_re=a_re, a_im=a_im, j=j):
            n0 = pl.multiple_of(g * SUBLANES, SUBLANES)
            h_re = h0re_ref[pl.ds(n0, SUBLANES), j * w:(j + 1) * w]
            h_im = h0im_ref[pl.ds(n0, SUBLANES), j * w:(j + 1) * w]
            for t in range(DEC_SEQ):
                r = pl.ds(pl.multiple_of(t * n_seq + n0, SUBLANES), SUBLANES)
                n_re = a_re * h_re - a_im * h_im + st_scr[r, re]
                n_im = a_re * h_im + a_im * h_re + st_scr[r, im]
                st_scr[r, re] = n_re
                st_scr[r, im] = n_im
                h_re, h_im = n_re, n_im
            sre_ref[pl.ds(n0, SUBLANES), j * w:(j + 1) * w] = h_re
            sim_ref[pl.ds(n0, SUBLANES), j * w:(j + 1) * w] = h_im
            return carry

        lax.fori_loop(0, n_seq // SUBLANES, seq_group, 0)

    up_scr[...] = _s5_readout(st_scr, up_scr, cbd_ref, d_ref, wglu_ref, n_slab)
    for j in range(n_slab):
        for t in range(DEC_SEQ):
            y_ref[j, pl.ds(t, n_seq, stride=DEC_SEQ), :] = (
                up_scr[t * n_seq:(t + 1) * n_seq, j * LANES:(j + 1) * LANES])


def _s5_sample(u_slab, y_slab, h0_re, h0_im, p, layer):
    n_slab, t, _ = u_slab.shape
    n_seq, n_state = h0_re.shape[1:]
    rows = n_seq * DEC_SEQ
    tail = pl.BlockSpec((n_slab, rows, LANES), lambda i: (0, (t - rows) // rows, 0))
    params = (h0_re, h0_im, p["bbd"], p["cbd"], p["ab"], p["d"], p["wglu"])
    return pl.pallas_call(
        _s5_sample_kernel,
        out_shape=(jax.ShapeDtypeStruct(y_slab.shape, F32),
                   jax.ShapeDtypeStruct((n_seq, n_state), F32),
                   jax.ShapeDtypeStruct((n_seq, n_state), F32)),
        grid=(1,),
        in_specs=[tail, pl.BlockSpec(memory_space=pl.ANY)]
        + [_layer_resident(a, layer) for a in params],
        out_specs=(tail, _resident((n_seq, n_state)), _resident((n_seq, n_state))),
        scratch_shapes=[pltpu.VMEM((rows, n_slab * LANES), F32),
                        pltpu.VMEM((rows, 2 * n_state), F32)],
        input_output_aliases={1: 0},
        compiler_params=_cparams("arbitrary"),
        name="s5_sample",
    )(u_slab, y_slab, *params)


def _subln(o, g_ref, post_scale):
    return _rms(o, g_ref[...], SUBLN_EPS) * post_scale


def _prompt_attention(lam, qi, q_ref, k_ref, v_ref, g_ref, o_ref, q2_scr, m_scr, l_scr, acc_scr,
                      blk, post_scale):
    q = q_ref[...]
    lane = lax.broadcasted_iota(jnp.int32, q.shape, 1)
    zero = jnp.zeros_like(q)
    q2_scr[0:blk] = jnp.where(lane < HEAD_DIM, q, zero)
    q2_scr[blk:2 * blk] = jnp.where(lane >= HEAD_DIM, q, zero)
    m_scr[...] = jnp.full_like(m_scr, NEG_BIG)
    l_scr[...] = jnp.zeros_like(l_scr)
    acc_scr[...] = jnp.zeros_like(acc_scr)

    n_rep = blk // LANES

    def block(kj, masked):
        r = pl.ds(pl.multiple_of(kj * blk, blk), blk)
        s = lax.dot_general(q2_scr[...], k_ref[r, :], (((1,), (1,)), ((), ())),
                            preferred_element_type=F32)
        if masked:
            qpos = lax.broadcasted_iota(jnp.int32, s.shape, 0) % blk
            kpos = lax.broadcasted_iota(jnp.int32, s.shape, 1)
            s = jnp.where(kpos <= qpos, s, NEG_BIG)
        m_old = m_scr[...]
        m_new = jnp.maximum(m_old, jnp.max(s, axis=-1, keepdims=True))
        alpha = jnp.exp2(m_old - m_new)
        p = jnp.exp2(s - jnp.concatenate([m_new] * n_rep, axis=-1))
        l_scr[...] = alpha * l_scr[...] + jnp.sum(p, axis=-1, keepdims=True)
        acc_scr[...] = alpha * acc_scr[...] + jnp.dot(p.astype(BF16), v_ref[r, :],
                                                      preferred_element_type=F32)
        m_scr[...] = m_new

    def full_block(kj, carry):
        block(kj, False)
        return carry

    lax.fori_loop(0, qi, full_block, 0)
    block(qi, True)

    o1 = acc_scr[0:blk] / l_scr[0:blk]
    o2 = acc_scr[blk:2 * blk] / l_scr[blk:2 * blk]
    o_ref[...] = _subln(o1 - lam * o2, g_ref, post_scale)


def _sample_attention(lam, q_ref, kn_ref, vn_ref, g_ref, kt_pages, v_pages, o_ref, kt_scr, v_scr,
                      post_scale):
    n_row = N_HEADS * 2 * DEC_SEQ

    for j in range(len(kt_pages)):
        cols = slice(j * PAGE_SIZE, (j + 1) * PAGE_SIZE)
        kt_scr[:, cols] = kt_pages[j][...].astype(BF16)
        for h in range(N_HEADS):
            v_scr[h, cols, :] = v_pages[j][pl.ds(h, PAGE_SIZE, stride=N_HEADS), :].astype(BF16)
    def pad_rows(x):
        pad = jnp.zeros((PAGE_SIZE - DEC_SEQ, x.shape[1]), F32)
        return jnp.concatenate([x, pad], axis=0).astype(BF16)

    k_new = pad_rows(kn_ref[...])

    qt = jnp.concatenate([q_ref[...]] * (n_row // DEC_SEQ), axis=0)
    rid = lax.broadcasted_iota(jnp.int32, qt.shape, 0)
    blk_id = lax.broadcasted_iota(jnp.int32, qt.shape, 1) // HEAD_DIM
    qm = jnp.where(blk_id == rid // DEC_SEQ, qt, 0.0).astype(BF16)

    s = jnp.dot(qm, kt_scr[...], preferred_element_type=F32)
    s_new = lax.dot_general(qm, k_new, (((1,), (1,)), ((), ())), preferred_element_type=F32)
    t_new = lax.broadcasted_iota(jnp.int32, s_new.shape, 1)
    q_idx = lax.broadcasted_iota(jnp.int32, s_new.shape, 0) % DEC_SEQ
    s_new = jnp.where(t_new <= q_idx, s_new, NEG_BIG)

    m = jnp.maximum(jnp.max(s, axis=-1, keepdims=True), jnp.max(s_new, axis=-1, keepdims=True))
    p = jnp.exp2(s - m)
    p_new = jnp.exp2(s_new - m)
    l = jnp.sum(p, axis=-1, keepdims=True) + jnp.sum(p_new, axis=-1, keepdims=True)
    pb = p.astype(BF16)
    pb_new = p_new.astype(BF16)
    for h in range(N_HEADS):
        rows = slice(2 * DEC_SEQ * h, 2 * DEC_SEQ * (h + 1))
        v_new = pad_rows(vn_ref[pl.ds(h, DEC_SEQ, stride=N_HEADS), :])
        o_h = (jnp.dot(pb[rows], v_scr[h], preferred_element_type=F32)
               + jnp.dot(pb_new[rows], v_new, preferred_element_type=F32)) / l[rows]
        o_ref[:, h * V_DIM:(h + 1) * V_DIM] = _subln(
            o_h[0:DEC_SEQ] - lam * o_h[DEC_SEQ:2 * DEC_SEQ], g_ref, post_scale)


def _attn_kernel(pt_ref, lam_ref, q_ref, k_ref, v_ref, g_ref, qs_ref, kn_ref, vn_ref, *rest,
                 blk, n_pages, n_seq, post_scale):
    del pt_ref
    kt_pages = rest[:n_pages]
    v_pages = rest[n_pages:2 * n_pages]
    o_ref, os_ref = rest[2 * n_pages:2 * n_pages + 2]
    q2_scr, m_scr, l_scr, acc_scr, kt_scr, v_scr = rest[2 * n_pages + 2:]
    lam = lam_ref[0, 0]
    qi = pl.program_id(2)
    step = (pl.program_id(0) * pl.num_programs(1) + pl.program_id(1)) * pl.num_programs(2) + qi
    _prompt_attention(lam, qi, q_ref, k_ref, v_ref, g_ref, o_ref, q2_scr, m_scr, l_scr, acc_scr,
                      blk, post_scale)

    @pl.when(step < n_seq)
    def _():
        _sample_attention(lam, qs_ref, kn_ref, vn_ref, g_ref, kt_pages, v_pages, os_ref,
                          kt_scr, v_scr, post_scale)


def _attention(page_table, lam, q_b, k_b, v_b, q_f, k_s, v_s, g_subln, kt_pool, v_pool, layer,
               n_batch, seq, post_scale):
    n_seq, n_pages = page_table.shape
    t, width = q_f.shape
    past = n_pages * PAGE_SIZE
    blk = _largest_tile(seq, 512)
    nq = seq // blk
    assert n_seq <= n_batch * N_HEADS * nq, "one sample sequence per grid step"
    first = t // DEC_SEQ - n_seq

    def sample(b, h, i):
        return jnp.minimum((b * N_HEADS + h) * nq + i, n_seq - 1)

    q_spec = pl.BlockSpec((blk, V_DIM), lambda b, h, i, pt: (b * nq + i, h))
    kv_spec = pl.BlockSpec((seq, V_DIM), lambda b, h, i, pt: (b, h))
    tok = pl.BlockSpec((DEC_SEQ, width), lambda b, h, i, pt: (first + sample(b, h, i), 0))
    new_k = pl.BlockSpec((None, DEC_SEQ, width), lambda b, h, i, pt: (layer, sample(b, h, i), 0))
    new_v = pl.BlockSpec((None, DEC_SEQ * N_HEADS, V_DIM),
                         lambda b, h, i, pt: (layer, sample(b, h, i), 0))

    def page_spec(shape, j):
        return pl.BlockSpec((None, None) + shape,
                            lambda b, h, i, pt, j=j: (layer, pt[sample(b, h, i), j], 0, 0))

    kt_specs = [page_spec(kt_pool.shape[2:], j) for j in range(n_pages)]
    v_specs = [page_spec(v_pool.shape[2:], j) for j in range(n_pages)]
    grid_spec = pltpu.PrefetchScalarGridSpec(
        num_scalar_prefetch=1,
        grid=(n_batch, N_HEADS, nq),
        in_specs=[pl.BlockSpec(memory_space=pltpu.SMEM), q_spec, kv_spec, kv_spec,
                  pl.BlockSpec((1, V_DIM), lambda b, h, i, pt: (0, 0)),
                  tok, new_k, new_v] + kt_specs + v_specs,
        out_specs=(q_spec,
                   pl.BlockSpec((DEC_SEQ, width), lambda b, h, i, pt: (sample(b, h, i), 0))),
        scratch_shapes=[pltpu.VMEM((2 * blk, V_DIM), BF16),
                        pltpu.VMEM((2 * blk, LANES), F32),
                        pltpu.VMEM((2 * blk, LANES), F32),
                        pltpu.VMEM((2 * blk, V_DIM), F32),
                        pltpu.VMEM((width, past), BF16),
                        pltpu.VMEM((N_HEADS, past, V_DIM), BF16)],
    )
    return pl.pallas_call(
        functools.partial(_attn_kernel, blk=blk, n_pages=n_pages, n_seq=n_seq,
                          post_scale=post_scale),
        out_shape=(jax.ShapeDtypeStruct((n_batch * seq, N_HEADS * V_DIM), F32),
                   jax.ShapeDtypeStruct((n_seq * DEC_SEQ, width), F32)),
        grid_spec=grid_spec,
        compiler_params=_cparams("arbitrary", "arbitrary", "arbitrary"),
        name="attention",
    )(page_table, lam, q_b, k_b, v_b, g_subln, q_f, k_s, v_s,
      *([kt_pool] * n_pages), *([v_pool] * n_pages))


def _merge_kernel(x_ref, ys_ref, yap_ref, yas_ref, gpre_ref, wgate_ref, wbs_ref, wba_ref, wo_ref,
                  g_ref, o_ref, *, n_prompt_tiles):
    d = x_ref.shape[1]
    ys = jnp.concatenate([ys_ref[j] for j in range(ys_ref.shape[0])], axis=-1).astype(BF16)
    ya = jnp.where(pl.program_id(0) < n_prompt_tiles, yap_ref[...], yas_ref[...]).astype(BF16)
    h = _rms(x_ref[...], gpre_ref[...], RMS_EPS).astype(BF16)
    gate = jnp.dot(h, wgate_ref[...], preferred_element_type=F32)
    merged = (jax.nn.sigmoid(gate[:, :d]) * jnp.dot(ys, wbs_ref[...], preferred_element_type=F32)
              + jax.nn.sigmoid(gate[:, d:]) * jnp.dot(ya, wba_ref[...], preferred_element_type=F32))
    out = jnp.dot(merged.astype(BF16), wo_ref[...], preferred_element_type=F32)
    o_ref[...] = x_ref[...] + _rms(out, g_ref[...], RMS_EPS)


def _merge(x, y_s5, y_att_p, y_att_s, g_pre, w_in, wbs, wba, wo, g_post, layer):
    t, d = x.shape
    tm = _largest_tile(math.gcd(y_att_p.shape[0], y_att_s.shape[0]), 512)
    np_tiles = y_att_p.shape[0] // tm
    n_slab = y_s5.shape[0]
    width = y_att_p.shape[1]

    def rows(width):
        return pl.BlockSpec((tm, width), lambda i: (i, 0))

    vec = pl.BlockSpec((1, d), lambda i: (0, 0))
    return pl.pallas_call(
        functools.partial(_merge_kernel, n_prompt_tiles=np_tiles),
        out_shape=jax.ShapeDtypeStruct((t, d), F32),
        grid=(t // tm,),
        in_specs=[rows(d), pl.BlockSpec((n_slab, tm, LANES), lambda i: (0, i, 0)),
                  pl.BlockSpec((tm, width), lambda i: (jnp.minimum(i, np_tiles - 1), 0)),
                  pl.BlockSpec((tm, width), lambda i: (jnp.maximum(i - np_tiles, 0), 0)), vec,
                  _layer_resident(w_in, layer, block=(d, 2 * d), index=(0, w_in.shape[2] // (2 * d) - 1)),
                  _layer_resident(wbs, layer), _layer_resident(wba, layer),
                  _layer_resident(wo, layer), vec],
        out_specs=rows(d),
        compiler_params=_cparams("arbitrary"),
        name="merge",
    )(x, y_s5, y_att_p, y_att_s, g_pre, w_in, wbs, wba, wo, g_post)


def _block_diag(m):
    eye = jnp.eye(GROUPS_PER_SLAB, dtype=m.dtype)
    depth, n_slab, g, r, c = m.shape
    return jnp.einsum("ljgrc,gh->ljgrhc", m, eye).reshape(depth, n_slab, g * r, g * c)


def _s5_params(a_re, a_im, log_dt, b_re, b_im, c_re, c_im, d, w_glu, tc):
    depth, groups, n_p = a_re.shape
    n_slab = groups // GROUPS_PER_SLAB
    seg = tc // SUBLANES
    dt = jnp.exp(log_dt)[:, :, None]
    mag = jnp.exp(dt * a_re)
    abar_re, abar_im = mag * jnp.cos(dt * a_im), mag * jnp.sin(dt * a_im)
    den = a_re * a_re + a_im * a_im
    inv_re, inv_im = a_re / den, -a_im / den
    fac_re = (abar_re - 1.0) * inv_re - abar_im * inv_im
    fac_im = (abar_re - 1.0) * inv_im + abar_im * inv_re
    fb_re = fac_re[..., None] * b_re - fac_im[..., None] * b_im
    fb_im = fac_re[..., None] * b_im + fac_im[..., None] * b_re

    def slab(x):
        return x.reshape((depth, n_slab, GROUPS_PER_SLAB) + x.shape[2:])

    bbd = jnp.concatenate([_block_diag(slab(fb_re).swapaxes(-1, -2)),
                           _block_diag(slab(fb_im).swapaxes(-1, -2))], axis=-1).astype(BF16)
    cbd = jnp.concatenate([_block_diag(slab(c_re).swapaxes(-1, -2)),
                           _block_diag(slab(-c_im).swapaxes(-1, -2))], axis=-2).astype(BF16)

    def lanes(x):
        return x.reshape(depth, n_slab, SLAB_STATES)

    k = jnp.arange(1, seg + 1, dtype=F32)[None, :, None, None]
    pw_mag = jnp.exp(k * (dt * a_re)[:, None])
    pw_arg = k * (dt * a_im)[:, None]
    pw_re, pw_im = pw_mag * jnp.cos(pw_arg), pw_mag * jnp.sin(pw_arg)

    def rows8(x):
        return jnp.broadcast_to(x[..., None, :], x.shape[:-1] + (SUBLANES, x.shape[-1]))

    def seg_lanes(x):
        return x.reshape(depth, seg, n_slab, SLAB_STATES).swapaxes(1, 2)

    ab = rows8(jnp.stack([lanes(abar_re), lanes(abar_im)], axis=2))
    aps = rows8(jnp.stack([lanes(pw_re[:, -1]), lanes(pw_im[:, -1])], axis=2))
    pw = rows8(jnp.stack([seg_lanes(pw_re), seg_lanes(pw_im)], axis=2))
    return dict(bbd=bbd, cbd=cbd, ab=ab, aps=aps, pw=pw,
                d=d[:, None, :], wglu=w_glu.astype(BF16))


def kernel(x_prompt, x_sample, cache_k, cache_v, state_s5_re, state_s5_im, page_table, g_ffn1_pre, g_ffn1_post, w_ffn1_gate, w_ffn1_up, w_ffn1_down, g_mix_pre, g_mix_post, w_in, s5_a_re, s5_a_im, s5_log_dt, s5_b_re, s5_b_im, s5_c_re, s5_c_im, s5_d, w_glu, lambda_q1, lambda_k1, lambda_q2, lambda_k2, g_subln, w_branch_s5, w_branch_att, w_out, g_ffn2_pre, g_ffn2_post, w_ffn2_gate, w_ffn2_up, w_ffn2_down):
    n_batch, seq, d_model = x_prompt.shape
    n_seq = x_sample.shape[0]
    depth = w_in.shape[0]
    n_prompt = n_batch * seq
    groups, n_p = s5_a_re.shape[1:]
    d_s5 = groups * S5_GROUP
    d_qk = N_HEADS * 2 * HEAD_DIM
    d_att = N_HEADS * V_DIM
    n_state = groups * n_p

    xs = (x_prompt.reshape(n_prompt, d_model), x_sample.reshape(n_seq * DEC_SEQ, d_model))

    tc = _largest_tile(seq, 512)
    ffn1 = [w.astype(BF16) for w in (w_ffn1_gate, w_ffn1_up, w_ffn1_down)]
    ffn2 = [w.astype(BF16) for w in (w_ffn2_gate, w_ffn2_up, w_ffn2_down)]
    w_in_b = w_in.astype(BF16)
    wbs_b, wba_b, wo_b = w_branch_s5.astype(BF16), w_branch_att.astype(BF16), w_out.astype(BF16)
    s5p = _s5_params(s5_a_re, s5_a_im, s5_log_dt, s5_b_re, s5_b_im, s5_c_re, s5_c_im,
                     s5_d, w_glu, tc)
    s5p["tc"] = tc
    lam_dyn = (jnp.exp(jnp.sum(lambda_q1 * lambda_k1, axis=-1))
               - jnp.exp(jnp.sum(lambda_q2 * lambda_k2, axis=-1)))
    n_phys = cache_k.shape[1]
    kt_pool = cache_k.transpose(0, 1, 3, 4, 5, 2).reshape(depth, n_phys, d_qk, PAGE_SIZE)
    v_pool = cache_v.reshape(depth, n_phys, PAGE_SIZE * N_HEADS, V_DIM)
    h0_re = state_s5_re.reshape(depth, n_seq, n_state)
    h0_im = state_s5_im.reshape(depth, n_seq, n_state)
    assert w_in.shape[2] == d_s5 + 2 * d_qk + d_att + 2 * d_model == 2 * (2 * d_model)

    def vec(g, l):
        return g[l][None, :]

    stacks = ()
    s5_states = [[] for _ in range(4)]
    for l in range(depth):
        lambda_init = 0.8 - 0.6 * math.exp(-0.3 * l)
        lam = (lam_dyn[l] + lambda_init).reshape(1, 1)

        x = _ffn(xs, vec(g_ffn1_pre, l), vec(g_ffn1_post, l), *ffn1, l, n_prompt, False)
        u, q_f, q_b, k_b, v_b, *stacks = _inproj(
            x, vec(g_mix_pre, l), w_in_b, stacks, l, depth, n_batch, seq, d_s5, d_qk, d_att)
        kt_p, k_s, v_p, v_s = stacks

        y, sp_re, sp_im = _s5_prompt(u, n_batch, seq, s5p, l)
        y, ss_re, ss_im = _s5_sample(u, y, h0_re, h0_im, s5p, l)
        g_sub = vec(g_subln, l)
        o_p, o_s = _attention(page_table, lam, q_b, k_b, v_b, q_f, k_s, v_s, g_sub, kt_pool, v_pool,
                              l, n_batch, seq, 1.0 - lambda_init)
        x = _merge(x, y, o_p, o_s, vec(g_mix_pre, l), w_in_b, wbs_b, wba_b, wo_b,
                   vec(g_mix_post, l), l)
        xs = _ffn((x,), vec(g_ffn2_pre, l), vec(g_ffn2_post, l), *ffn2, l, n_prompt,
                  l == depth - 1)
        xs = xs if l == depth - 1 else (xs,)

        for lst, val in zip(s5_states, (sp_re, sp_im, ss_re, ss_im)):
            lst.append(val)

    sp_re, sp_im, ss_re, ss_im = [jnp.stack(s) for s in s5_states]
    k_p = kt_p.reshape(depth, n_batch, N_HEADS, 2, HEAD_DIM, seq).transpose(0, 1, 5, 2, 3, 4)
    return (xs[0].reshape(n_batch, seq, d_model),
            xs[1].reshape(n_seq, DEC_SEQ, d_model),
            k_p,
            v_p.reshape(depth, n_batch, seq, N_HEADS, V_DIM),
            sp_re.reshape(depth, n_batch, groups, n_p),
            sp_im.reshape(depth, n_batch, groups, n_p),
            k_s.reshape(depth, n_seq, DEC_SEQ, N_HEADS, 2, HEAD_DIM),
            v_s.reshape(depth, n_seq, DEC_SEQ, N_HEADS, V_DIM),
            ss_re.reshape(depth, n_seq, groups, n_p),
            ss_im.reshape(depth, n_seq, groups, n_p))
```

```python
import functools
import math

import jax
import jax.numpy as jnp
from jax import lax
from jax.experimental import pallas as pl
from jax.experimental.pallas import tpu as pltpu

F32 = jnp.float32
BF16 = jnp.bfloat16

PAGE_SIZE = 128
DEC_SEQ = 8
S5_GROUP = 16
S5_STATE = 64
N_HEADS = 4
HEAD_DIM = 64
V_DIM = 2 * HEAD_DIM
RMS_EPS = 1e-6
SUBLN_EPS = 1e-5
NEG_BIG = -1e30
Q_SCALE = HEAD_DIM ** -0.5 * math.log2(math.e)

LANES = 128
SUBLANES = 8
GROUPS_PER_SLAB = LANES // S5_GROUP
SLAB_STATES = GROUPS_PER_SLAB * S5_STATE
FF_CHUNK = 256
VMEM_LIMIT = 56 * 1024 * 1024
FFN_ROWS = 1024
TOKEN_ROWS = 512
ATTN_BLOCK = 512
S5_CHUNK = 512


def _largest_tile(n, pref):
    t = pref
    while n % t:
        t //= 2
    return t


def _rms(x, g, eps):
    return x * lax.rsqrt(jnp.mean(x * x, axis=-1, keepdims=True) + eps) * g


def _cparams(*sem):
    return pltpu.CompilerParams(dimension_semantics=sem, vmem_limit_bytes=VMEM_LIMIT)


def _resident(shape):
    nd = len(shape)
    return pl.BlockSpec(shape, lambda *_: (0,) * nd, pipeline_mode=pl.Buffered(1))


def _layer_resident(stack, layer, block=None, index=None):
    block = tuple(stack.shape[1:]) if block is None else block
    index = (0,) * len(block) if index is None else index
    return pl.BlockSpec((None,) + block, lambda *_: (layer,) + index, pipeline_mode=pl.Buffered(1))


def _ffn_kernel(*refs, n_x, n_out, n_prompt_tiles):
    x_refs = refs[:n_x]
    gpre_ref, gpost_ref, wg_ref, wu_ref, wd_ref = refs[n_x:n_x + 5]
    o_refs = refs[n_x + 5:n_x + 5 + n_out]
    h_scr, acc_scr = refs[n_x + 5 + n_out:]
    is_prompt = pl.program_id(0) < n_prompt_tiles
    x = x_refs[0][...] if n_x == 1 else jnp.where(is_prompt, x_refs[0][...], x_refs[1][...])
    h_scr[...] = _rms(x, gpre_ref[...], RMS_EPS).astype(BF16)
    for c in range(wg_ref.shape[1] // FF_CHUNK):
        cols = slice(c * FF_CHUNK, (c + 1) * FF_CHUNK)
        g = jnp.dot(h_scr[...], wg_ref[:, cols], preferred_element_type=F32)
        u = jnp.dot(h_scr[...], wu_ref[:, cols], preferred_element_type=F32)
        a = (g * jax.nn.sigmoid(g) * u).astype(BF16)
        part = jnp.dot(a, wd_ref[cols, :], preferred_element_type=F32)
        if c == 0:
            acc_scr[...] = part
        else:
            acc_scr[...] += part
    y = x + 0.5 * _rms(acc_scr[...], gpost_ref[...], RMS_EPS)
    if n_out == 1:
        o_refs[0][...] = y
    else:
        @pl.when(is_prompt)
        def _():
            o_refs[0][...] = y

        @pl.when(jnp.logical_not(is_prompt))
        def _():
            o_refs[1][...] = y


def _ffn(xs, gpre, gpost, wg, wu, wd, layer, n_prompt, split_out):
    d = xs[0].shape[1]
    t = sum(x.shape[0] for x in xs)
    tm = _largest_tile(math.gcd(n_prompt, t - n_prompt), FFN_ROWS)
    np_tiles = n_prompt // tm
    row = pl.BlockSpec((tm, d), lambda i: (i, 0))
    p_row = pl.BlockSpec((tm, d), lambda i: (jnp.minimum(i, np_tiles - 1), 0))
    s_row = pl.BlockSpec((tm, d), lambda i: (jnp.maximum(i - np_tiles, 0), 0))
    vec = pl.BlockSpec((1, d), lambda i: (0, 0))
    out_shape = ((jax.ShapeDtypeStruct((n_prompt, d), F32),
                  jax.ShapeDtypeStruct((t - n_prompt, d), F32)) if split_out
                 else jax.ShapeDtypeStruct((t, d), F32))
    return pl.pallas_call(
        functools.partial(_ffn_kernel, n_x=len(xs), n_out=2 if split_out else 1,
                          n_prompt_tiles=np_tiles),
        out_shape=out_shape,
        grid=(t // tm,),
        in_specs=([row] if len(xs) == 1 else [p_row, s_row]) + [vec, vec]
        + [_layer_resident(w, layer) for w in (wg, wu, wd)],
        out_specs=(p_row, s_row) if split_out else row,
        scratch_shapes=[pltpu.VMEM((tm, d), BF16), pltpu.VMEM((tm, d), F32)],
        compiler_params=_cparams("arbitrary"),
        name="ffn",
    )(*xs, gpre, gpost, wg, wu, wd)


def _inproj_kernel(x_ref, g_ref, w_ref, *rest, d_s5, d_qk, d_att, n_prompt_tiles):
    u_ref, qf_ref, qb_ref, kb_ref, vb_ref, ktp_ref, ks_ref, vp_ref, vs_ref = rest[-9:]
    i = pl.program_id(0)
    h = _rms(x_ref[...], g_ref[...], RMS_EPS).astype(BF16)

    def proj(lo, width):
        return jnp.dot(h, w_ref[:, lo:lo + width], preferred_element_type=F32)

    u = proj(0, d_s5)
    for j in range(d_s5 // LANES):
        u_ref[j] = u[:, j * LANES:(j + 1) * LANES]
    q = proj(d_s5, d_qk) * Q_SCALE
    qf_ref[...] = q
    qb_ref[...] = q.astype(BF16)
    k = proj(d_s5 + d_qk, d_qk)
    kb_ref[...] = k.astype(BF16)
    v = proj(d_s5 + 2 * d_qk, d_att)
    vb_ref[...] = v.astype(BF16)

    def store_heads(v_ref):
        for hd in range(N_HEADS):
            v_ref[pl.ds(hd, v.shape[0], stride=N_HEADS), :] = v[:, hd * V_DIM:(hd + 1) * V_DIM]

    @pl.when(i < n_prompt_tiles)
    def _():
        ktp_ref[...] = k.T
        store_heads(vp_ref)

    @pl.when(i >= n_prompt_tiles)
    def _():
        ks_ref[...] = k
        store_heads(vs_ref)


def _inproj(x, g, w, stacks, layer, depth, n_batch, seq, d_s5, d_qk, d_att):
    t, d = x.shape
    tm = _largest_tile(math.gcd(seq, t - n_batch * seq), TOKEN_ROWS)
    np_tiles = n_batch * seq // tm
    per_batch = seq // tm
    rows_s = t - n_batch * seq
    d_proj = d_s5 + 2 * d_qk + d_att

    def rows(width):
        return pl.BlockSpec((tm, width), lambda i: (i, 0))

    def p_tile(i):
        return jnp.minimum(i, np_tiles - 1)

    def s_tile(i):
        return jnp.maximum(i - np_tiles, 0)

    n_slab = d_s5 // LANES
    any_spec = pl.BlockSpec(memory_space=pl.ANY)
    n_in = 3
    return pl.pallas_call(
        functools.partial(_inproj_kernel, d_s5=d_s5, d_qk=d_qk, d_att=d_att,
                          n_prompt_tiles=np_tiles),
        out_shape=(jax.ShapeDtypeStruct((n_slab, t, LANES), F32),
                   jax.ShapeDtypeStruct((t, d_qk), F32),
                   jax.ShapeDtypeStruct((t, d_qk), BF16),
                   jax.ShapeDtypeStruct((t, d_qk), BF16),
                   jax.ShapeDtypeStruct((t, d_att), BF16),
                   jax.ShapeDtypeStruct((depth, n_batch, d_qk, seq), F32),
                   jax.ShapeDtypeStruct((depth, rows_s, d_qk), F32),
                   jax.ShapeDtypeStruct((depth, n_batch * seq * N_HEADS, V_DIM), F32),
                   jax.ShapeDtypeStruct((depth, rows_s * N_HEADS, V_DIM), F32)),
        grid=(t // tm,),
        in_specs=[rows(d), pl.BlockSpec((1, d), lambda i: (0, 0)),
                  _layer_resident(w, layer, block=(d, d_proj))] + [any_spec] * len(stacks),
        out_specs=(pl.BlockSpec((n_slab, tm, LANES), lambda i: (0, i, 0)),
                   rows(d_qk), rows(d_qk), rows(d_qk), rows(d_att),
                   pl.BlockSpec((None, None, d_qk, tm),
                                lambda i: (layer, p_tile(i) // per_batch, 0, p_tile(i) % per_batch)),
                   pl.BlockSpec((None, tm, d_qk), lambda i: (layer, s_tile(i), 0)),
                   pl.BlockSpec((None, tm * N_HEADS, V_DIM), lambda i: (layer, p_tile(i), 0)),
                   pl.BlockSpec((None, tm * N_HEADS, V_DIM), lambda i: (layer, s_tile(i), 0))),
        input_output_aliases={n_in + s: 5 + s for s in range(len(stacks))},
        compiler_params=_cparams("arbitrary"),
        name="in_proj",
    )(x, g, w, *stacks)


def _s5_readout(st_scr, up_scr, cbd_ref, d_ref, wglu_ref, n_slab):
    ys = []
    for j in range(n_slab):
        hj = st_scr[:, 2 * SLAB_STATES * j:2 * SLAB_STATES * (j + 1)].astype(BF16)
        ys.append(jnp.dot(hj, cbd_ref[j], preferred_element_type=F32))
    y = jnp.concatenate(ys, axis=-1) + d_ref[...] * up_scr[...]
    z = jax.nn.gelu(y).astype(BF16)
    zg = jnp.dot(z, wglu_ref[...], preferred_element_type=F32)
    half = zg.shape[1] // 2
    return zg[:, :half] * jax.nn.sigmoid(zg[:, half:])


def _s5_input(st_scr, up_scr, bbd_ref, n_slab):
    for j in range(n_slab):
        uj = up_scr[:, j * LANES:(j + 1) * LANES].astype(BF16)
        st_scr[:, 2 * SLAB_STATES * j:2 * SLAB_STATES * (j + 1)] = jnp.dot(
            uj, bbd_ref[j], preferred_element_type=F32)


def _s5_prompt_kernel(u_ref, bbd_ref, cbd_ref, ab_ref, aps_ref, pw_ref, d_ref, wglu_ref,
                      y_ref, sre_ref, sim_ref, up_scr, st_scr, carry_scr, *, tc):
    c = pl.program_id(1)
    n_slab = u_ref.shape[0]
    seg = tc // SUBLANES
    w = SLAB_STATES

    @pl.when(c == 0)
    def _():
        carry_scr[...] = jnp.zeros_like(carry_scr)

    for j in range(n_slab):
        for k in range(seg):
            up_scr[k * SUBLANES:(k + 1) * SUBLANES, j * LANES:(j + 1) * LANES] = (
                u_ref[j, pl.ds(k, SUBLANES, stride=seg), :])
    _s5_input(st_scr, up_scr, bbd_ref, n_slab)

    row_id = lax.broadcasted_iota(jnp.int32, (SUBLANES, w), 0)
    for j in range(n_slab):
        re = slice(2 * w * j, 2 * w * j + w)
        im = slice(2 * w * j + w, 2 * w * (j + 1))
        a_re = ab_ref[j, 0]
        a_im = ab_ref[j, 1]

        def scan_step(k, h, re=re, im=im, a_re=a_re, a_im=a_im):
            h_re, h_im = h
            r = pl.ds(pl.multiple_of(k * SUBLANES, SUBLANES), SUBLANES)
            n_re = a_re * h_re - a_im * h_im + st_scr[r, re]
            n_im = a_re * h_im + a_im * h_re + st_scr[r, im]
            st_scr[r, re] = n_re
            st_scr[r, im] = n_im
            return n_re, n_im

        zero = jnp.zeros((SUBLANES, w), F32)
        e_re, e_im = lax.fori_loop(0, seg, scan_step, (zero, zero), unroll=True)

        s_re = aps_ref[j, 0][0:1]
        s_im = aps_ref[j, 1][0:1]
        cur_re = carry_scr[0:1, re]
        cur_im = carry_scr[0:1, im]
        c_re = jnp.broadcast_to(cur_re, (SUBLANES, w))
        c_im = jnp.broadcast_to(cur_im, (SUBLANES, w))
        for i in range(1, SUBLANES + 1):
            nxt_re = s_re * cur_re - s_im * cur_im + e_re[i - 1:i]
            nxt_im = s_re * cur_im + s_im * cur_re + e_im[i - 1:i]
            cur_re, cur_im = nxt_re, nxt_im
            if i < SUBLANES:
                c_re = jnp.where(row_id == i, cur_re, c_re)
                c_im = jnp.where(row_id == i, cur_im, c_im)
        carry_scr[0:1, re] = cur_re
        carry_scr[0:1, im] = cur_im

        def fix_step(k, carry, re=re, im=im, c_re=c_re, c_im=c_im, j=j):
            r = pl.ds(pl.multiple_of(k * SUBLANES, SUBLANES), SUBLANES)
            p_re = pw_ref[j, 0, k]
            p_im = pw_ref[j, 1, k]
            st_scr[r, re] += p_re * c_re - p_im * c_im
            st_scr[r, im] += p_re * c_im + p_im * c_re
            return carry

        lax.fori_loop(0, seg, fix_step, 0, unroll=True)

    up_scr[...] = _s5_readout(st_scr, up_scr, cbd_ref, d_ref, wglu_ref, n_slab)
    for j in range(n_slab):
        for k in range(seg):
            y_ref[j, pl.ds(k, SUBLANES, stride=seg), :] = (
                up_scr[k * SUBLANES:(k + 1) * SUBLANES, j * LANES:(j + 1) * LANES])

    @pl.when(c == pl.num_programs(1) - 1)
    def _():
        for j in range(n_slab):
            sre_ref[0, :, j * w:(j + 1) * w] = carry_scr[0:1, 2 * w * j:2 * w * j + w]
            sim_ref[0, :, j * w:(j + 1) * w] = carry_scr[0:1, 2 * w * j + w:2 * w * (j + 1)]


def _s5_prompt(u_slab, n_batch, seq, p, layer):
    n_slab = u_slab.shape[0]
    d_s5 = n_slab * LANES
    tc = p["tc"]
    n_chunks = seq // tc
    n_state = n_slab * SLAB_STATES
    u_spec = pl.BlockSpec((n_slab, tc, LANES), lambda b, c: (0, b * n_chunks + c, 0))
    st_spec = pl.BlockSpec((1, 1, n_state), lambda b, c: (b, 0, 0))
    return pl.pallas_call(
        functools.partial(_s5_prompt_kernel, tc=tc),
        out_shape=(jax.ShapeDtypeStruct(u_slab.shape, F32),
                   jax.ShapeDtypeStruct((n_batch, 1, n_state), F32),
                   jax.ShapeDtypeStruct((n_batch, 1, n_state), F32)),
        grid=(n_batch, n_chunks),
        in_specs=[u_spec] + [_layer_resident(p[k], layer)
                             for k in ("bbd", "cbd", "ab", "aps", "pw", "d", "wglu")],
        out_specs=(u_spec, st_spec, st_spec),
        scratch_shapes=[pltpu.VMEM((tc, d_s5), F32),
                        pltpu.VMEM((tc, 2 * n_state), F32),
                        pltpu.VMEM((SUBLANES, 2 * n_state), F32)],
        compiler_params=_cparams("arbitrary", "arbitrary"),
        name="s5_prompt",
    )(u_slab, p["bbd"], p["cbd"], p["ab"], p["aps"], p["pw"], p["d"], p["wglu"])


def _s5_sample_kernel(u_ref, y_alias_ref, h0re_ref, h0im_ref, bbd_ref, cbd_ref, ab_ref, d_ref,
                      wglu_ref, y_ref, sre_ref, sim_ref, up_scr, st_scr):
    del y_alias_ref
    n_slab = u_ref.shape[0]
    n_seq = h0re_ref.shape[0]
    w = SLAB_STATES
    for j in range(n_slab):
        for t in range(DEC_SEQ):
            up_scr[t * n_seq:(t + 1) * n_seq, j * LANES:(j + 1) * LANES] = (
                u_ref[j, pl.ds(t, n_seq, stride=DEC_SEQ), :])
    _s5_input(st_scr, up_scr, bbd_ref, n_slab)

    for j in range(n_slab):
        re = slice(2 * w * j, 2 * w * j + w)
        im = slice(2 * w * j + w, 2 * w * (j + 1))
        a_re = ab_ref[j, 0]
        a_im = ab_ref[j, 1]

        def seq_group(g, carry, re=re, im=im, a_re=a_re, a_im=a_im, j=j):
            n0 = pl.multiple_of(g * SUBLANES, SUBLANES)
            h_re = h0re_ref[pl.ds(n0, SUBLANES), j * w:(j + 1) * w]
            h_im = h0im_ref[pl.ds(n0, SUBLANES), j * w:(j + 1) * w]
            for t in range(DEC_SEQ):
                r = pl.ds(pl.multiple_of(t * n_seq + n0, SUBLANES), SUBLANES)
                n_re = a_re * h_re - a_im * h_im + st_scr[r, re]
                n_im = a_re * h_im + a_im * h_re + st_scr[r, im]
                st_scr[r, re] = n_re
                st_scr[r, im] = n_im
                h_re, h_im = n_re, n_im
            sre_ref[pl.ds(n0, SUBLANES), j * w:(j + 1) * w] = h_re
            sim_ref[pl.ds(n0, SUBLANES), j * w:(j + 1) * w] = h_im
            return carry

        lax.fori_loop(0, n_seq // SUBLANES, seq_group, 0)

    up_scr[...] = _s5_readout(st_scr, up_scr, cbd_ref, d_ref, wglu_ref, n_slab)
    for j in range(n_slab):
        for t in range(DEC_SEQ):
            y_ref[j, pl.ds(t, n_seq, stride=DEC_SEQ), :] = (
                up_scr[t * n_seq:(t + 1) * n_seq, j * LANES:(j + 1) * LANES])


def _s5_sample(u_slab, y_slab, h0_re, h0_im, p, layer):
    n_slab, t, _ = u_slab.shape
    n_seq, n_state = h0_re.shape[1:]
    rows = n_seq * DEC_SEQ
    tail = pl.BlockSpec((n_slab, rows, LANES), lambda i: (0, (t - rows) // rows, 0))
    params = (h0_re, h0_im, p["bbd"], p["cbd"], p["ab"], p["d"], p["wglu"])
    return pl.pallas_call(
        _s5_sample_kernel,
        out_shape=(jax.ShapeDtypeStruct(y_slab.shape, F32),
                   jax.ShapeDtypeStruct((n_seq, n_state), F32),
                   jax.ShapeDtypeStruct((n_seq, n_state), F32)),
        grid=(1,),
        in_specs=[tail, pl.BlockSpec(memory_space=pl.ANY)]
        + [_layer_resident(a, layer) for a in params],
        out_specs=(tail, _resident((n_seq, n_state)), _resident((n_seq, n_state))),
        scratch_shapes=[pltpu.VMEM((rows, n_slab * LANES), F32),
                        pltpu.VMEM((rows, 2 * n_state), F32)],
        input_output_aliases={1: 0},
        compiler_params=_cparams("arbitrary"),
        name="s5_sample",
    )(u_slab, y_slab, *params)


def _subln(o, g_ref, post_scale):
    return _rms(o, g_ref[...], SUBLN_EPS) * post_scale


def _prompt_attention(lam, qi, q_ref, k_ref, v_ref, g_ref, o_ref, q2_scr, m_scr, l_scr, acc_scr,
                      blk, post_scale):
    q = q_ref[...]
    lane = lax.broadcasted_iota(jnp.int32, q.shape, 1)
    zero = jnp.zeros_like(q)
    q2_scr[0:blk] = jnp.where(lane < HEAD_DIM, q, zero)
    q2_scr[blk:2 * blk] = jnp.where(lane >= HEAD_DIM, q, zero)
    m_scr[...] = jnp.full_like(m_scr, NEG_BIG)
    l_scr[...] = jnp.zeros_like(l_scr)
    acc_scr[...] = jnp.zeros_like(acc_scr)

    n_rep = blk // LANES

    def block(kj, masked):
        r = pl.ds(pl.multiple_of(kj * blk, blk), blk)
        s = lax.dot_general(q2_scr[...], k_ref[r, :], (((1,), (1,)), ((), ())),
                            preferred_element_type=F32)
        if masked:
            qpos = lax.broadcasted_iota(jnp.int32, s.shape, 0) % blk
            kpos = lax.broadcasted_iota(jnp.int32, s.shape, 1)
            s = jnp.where(kpos <= qpos, s, NEG_BIG)
        m_old = m_scr[...]
        m_new = jnp.maximum(m_old, jnp.max(s, axis=-1, keepdims=True))
        alpha = jnp.exp2(m_old - m_new)
        p = jnp.exp2(s - jnp.concatenate([m_new] * n_rep, axis=-1))
        l_scr[...] = alpha * l_scr[...] + jnp.sum(p, axis=-1, keepdims=True)
        acc_scr[...] = alpha * acc_scr[...] + jnp.dot(p.astype(BF16), v_ref[r, :],
                                                      preferred_element_type=F32)
        m_scr[...] = m_new

    def full_block(kj, carry):
        block(kj, False)
        return carry

    lax.fori_loop(0, qi, full_block, 0)
    block(qi, True)

    o1 = acc_scr[0:blk] / l_scr[0:blk]
    o2 = acc_scr[blk:2 * blk] / l_scr[blk:2 * blk]
    o_ref[...] = _subln(o1 - lam * o2, g_ref, post_scale)


def _sample_attention(lam, q_ref, kn_ref, vn_ref, g_ref, kt_pages, v_pages, o_ref, kt_scr, v_scr,
                      post_scale):
    n_row = N_HEADS * 2 * DEC_SEQ

    for j in range(len(kt_pages)):
        cols = slice(j * PAGE_SIZE, (j + 1) * PAGE_SIZE)
        kt_scr[:, cols] = kt_pages[j][...].astype(BF16)
        for h in range(N_HEADS):
            v_scr[h, cols, :] = v_pages[j][pl.ds(h, PAGE_SIZE, stride=N_HEADS), :].astype(BF16)
    def pad_rows(x):
        pad = jnp.zeros((PAGE_SIZE - DEC_SEQ, x.shape[1]), F32)
        return jnp.concatenate([x, pad], axis=0).astype(BF16)

    k_new = pad_rows(kn_ref[...])

    qt = jnp.concatenate([q_ref[...]] * (n_row // DEC_SEQ), axis=0)
    rid = lax.broadcasted_iota(jnp.int32, qt.shape, 0)
    blk_id = lax.broadcasted_iota(jnp.int32, qt.shape, 1) // HEAD_DIM
    qm = jnp.where(blk_id == rid // DEC_SEQ, qt, 0.0).astype(BF16)

    s = jnp.dot(qm, kt_scr[...], preferred_element_type=F32)
    s_new = lax.dot_general(qm, k_new, (((1,), (1,)), ((), ())), preferred_element_type=F32)
    t_new = lax.broadcasted_iota(jnp.int32, s_new.shape, 1)
    q_idx = lax.broadcasted_iota(jnp.int32, s_new.shape, 0) % DEC_SEQ
    s_new = jnp.where(t_new <= q_idx, s_new, NEG_BIG)

    m = jnp.maximum(jnp.max(s, axis=-1, keepdims=True), jnp.max(s_new, axis=-1, keepdims=True))
    p = jnp.exp2(s - m)
    p_new = jnp.exp2(s_new - m)
    l = jnp.sum(p, axis=-1, keepdims=True) + jnp.sum(p_new, axis=-1, keepdims=True)
    pb = p.astype(BF16)
    pb_new = p_new.astype(BF16)
    for h in range(N_HEADS):
        rows = slice(2 * DEC_SEQ * h, 2 * DEC_SEQ * (h + 1))
        v_new = pad_rows(vn_ref[pl.ds(h, DEC_SEQ, stride=N_HEADS), :])
        o_h = (jnp.dot(pb[rows], v_scr[h], preferred_element_type=F32)
               + jnp.dot(pb_new[rows], v_new, preferred_element_type=F32)) / l[rows]
        o_ref[:, h * V_DIM:(h + 1) * V_DIM] = _subln(
            o_h[0:DEC_SEQ] - lam * o_h[DEC_SEQ:2 * DEC_SEQ], g_ref, post_scale)


def _attn_kernel(pt_ref, lam_ref, q_ref, k_ref, v_ref, g_ref, qs_ref, kn_ref, vn_ref, *rest,
                 blk, n_pages, n_seq, post_scale):
    del pt_ref
    kt_pages = rest[:n_pages]
    v_pages = rest[n_pages:2 * n_pages]
    o_ref, os_ref = rest[2 * n_pages:2 * n_pages + 2]
    q2_scr, m_scr, l_scr, acc_scr, kt_scr, v_scr = rest[2 * n_pages + 2:]
    lam = lam_ref[0, 0]
    qi = pl.program_id(2)
    step = (pl.program_id(0) * pl.num_programs(1) + pl.program_id(1)) * pl.num_programs(2) + qi
    _prompt_attention(lam, qi, q_ref, k_ref, v_ref, g_ref, o_ref, q2_scr, m_scr, l_scr, acc_scr,
                      blk, post_scale)

    @pl.when(step < n_seq)
    def _():
        _sample_attention(lam, qs_ref, kn_ref, vn_ref, g_ref, kt_pages, v_pages, os_ref,
                          kt_scr, v_scr, post_scale)


def _attention(page_table, lam, q_b, k_b, v_b, q_f, k_s, v_s, g_subln, kt_pool, v_pool, layer,
               n_batch, seq, post_scale):
    n_seq, n_pages = page_table.shape
    t, width = q_f.shape
    past = n_pages * PAGE_SIZE
    blk = _largest_tile(seq, ATTN_BLOCK)
    nq = seq // blk
    assert n_seq <= n_batch * N_HEADS * nq, "one sample sequence per grid step"
    first = t // DEC_SEQ - n_seq

    def sample(b, h, i):
        return jnp.minimum((b * N_HEADS + h) * nq + i, n_seq - 1)

    q_spec = pl.BlockSpec((blk, V_DIM), lambda b, h, i, pt: (b * nq + i, h))
    kv_spec = pl.BlockSpec((seq, V_DIM), lambda b, h, i, pt: (b, h))
    tok = pl.BlockSpec((DEC_SEQ, width), lambda b, h, i, pt: (first + sample(b, h, i), 0))
    new_k = pl.BlockSpec((None, DEC_SEQ, width), lambda b, h, i, pt: (layer, sample(b, h, i), 0))
    new_v = pl.BlockSpec((None, DEC_SEQ * N_HEADS, V_DIM),
                         lambda b, h, i, pt: (layer, sample(b, h, i), 0))

    def page_spec(shape, j):
        return pl.BlockSpec((None, None) + shape,
                            lambda b, h, i, pt, j=j: (layer, pt[sample(b, h, i), j], 0, 0))

    kt_specs = [page_spec(kt_pool.shape[2:], j) for j in range(n_pages)]
    v_specs = [page_spec(v_pool.shape[2:], j) for j in range(n_pages)]
    grid_spec = pltpu.PrefetchScalarGridSpec(
        num_scalar_prefetch=1,
        grid=(n_batch, N_HEADS, nq),
        in_specs=[pl.BlockSpec(memory_space=pltpu.SMEM), q_spec, kv_spec, kv_spec,
                  pl.BlockSpec((1, V_DIM), lambda b, h, i, pt: (0, 0)),
                  tok, new_k, new_v] + kt_specs + v_specs,
        out_specs=(q_spec,
                   pl.BlockSpec((DEC_SEQ, width), lambda b, h, i, pt: (sample(b, h, i), 0))),
        scratch_shapes=[pltpu.VMEM((2 * blk, V_DIM), BF16),
                        pltpu.VMEM((2 * blk, LANES), F32),
                        pltpu.VMEM((2 * blk, LANES), F32),
                        pltpu.VMEM((2 * blk, V_DIM), F32),
                        pltpu.VMEM((width, past), BF16),
                        pltpu.VMEM((N_HEADS, past, V_DIM), BF16)],
    )
    return pl.pallas_call(
        functools.partial(_attn_kernel, blk=blk, n_pages=n_pages, n_seq=n_seq,
                          post_scale=post_scale),
        out_shape=(jax.ShapeDtypeStruct((n_batch * seq, N_HEADS * V_DIM), F32),
                   jax.ShapeDtypeStruct((n_seq * DEC_SEQ, width), F32)),
        grid_spec=grid_spec,
        compiler_params=_cparams("arbitrary", "arbitrary", "arbitrary"),
        name="attention",
    )(page_table, lam, q_b, k_b, v_b, g_subln, q_f, k_s, v_s,
      *([kt_pool] * n_pages), *([v_pool] * n_pages))


def _merge_kernel(x_ref, ys_ref, yap_ref, yas_ref, gpre_ref, wgate_ref, wbs_ref, wba_ref, wo_ref,
                  g_ref, o_ref, *, n_prompt_tiles):
    d = x_ref.shape[1]
    ys = jnp.concatenate([ys_ref[j] for j in range(ys_ref.shape[0])], axis=-1).astype(BF16)
    ya = jnp.where(pl.program_id(0) < n_prompt_tiles, yap_ref[...], yas_ref[...]).astype(BF16)
    h = _rms(x_ref[...], gpre_ref[...], RMS_EPS).astype(BF16)
    gate = jnp.dot(h, wgate_ref[...], preferred_element_type=F32)
    merged = (jax.nn.sigmoid(gate[:, :d]) * jnp.dot(ys, wbs_ref[...], preferred_element_type=F32)
              + jax.nn.sigmoid(gate[:, d:]) * jnp.dot(ya, wba_ref[...], preferred_element_type=F32))
    out = jnp.dot(merged.astype(BF16), wo_ref[...], preferred_element_type=F32)
    o_ref[...] = x_ref[...] + _rms(out, g_ref[...], RMS_EPS)


def _merge(x, y_s5, y_att_p, y_att_s, g_pre, w_in, wbs, wba, wo, g_post, layer):
    t, d = x.shape
    tm = _largest_tile(math.gcd(y_att_p.shape[0], y_att_s.shape[0]), TOKEN_ROWS)
    np_tiles = y_att_p.shape[0] // tm
    n_slab = y_s5.shape[0]
    width = y_att_p.shape[1]

    def rows(width):
        return pl.BlockSpec((tm, width), lambda i: (i, 0))

    vec = pl.BlockSpec((1, d), lambda i: (0, 0))
    return pl.pallas_call(
        functools.partial(_merge_kernel, n_prompt_tiles=np_tiles),
        out_shape=jax.ShapeDtypeStruct((t, d), F32),
        grid=(t // tm,),
        in_specs=[rows(d), pl.BlockSpec((n_slab, tm, LANES), lambda i: (0, i, 0)),
                  pl.BlockSpec((tm, width), lambda i: (jnp.minimum(i, np_tiles - 1), 0)),
                  pl.BlockSpec((tm, width), lambda i: (jnp.maximum(i - np_tiles, 0), 0)), vec,
                  _layer_resident(w_in, layer, block=(d, 2 * d), index=(0, w_in.shape[2] // (2 * d) - 1)),
                  _layer_resident(wbs, layer), _layer_resident(wba, layer),
                  _layer_resident(wo, layer), vec],
        out_specs=rows(d),
        compiler_params=_cparams("arbitrary"),
        name="merge",
    )(x, y_s5, y_att_p, y_att_s, g_pre, w_in, wbs, wba, wo, g_post)


def _block_diag(m):
    eye = jnp.eye(GROUPS_PER_SLAB, dtype=m.dtype)
    depth, n_slab, g, r, c = m.shape
    return jnp.einsum("ljgrc,gh->ljgrhc", m, eye).reshape(depth, n_slab, g * r, g * c)


def _s5_params(a_re, a_im, log_dt, b_re, b_im, c_re, c_im, d, w_glu, tc):
    depth, groups, n_p = a_re.shape
    n_slab = groups // GROUPS_PER_SLAB
    seg = tc // SUBLANES
    dt = jnp.exp(log_dt)[:, :, None]
    mag = jnp.exp(dt * a_re)
    abar_re, abar_im = mag * jnp.cos(dt * a_im), mag * jnp.sin(dt * a_im)
    den = a_re * a_re + a_im * a_im
    inv_re, inv_im = a_re / den, -a_im / den
    fac_re = (abar_re - 1.0) * inv_re - abar_im * inv_im
    fac_im = (abar_re - 1.0) * inv_im + abar_im * inv_re
    fb_re = fac_re[..., None] * b_re - fac_im[..., None] * b_im
    fb_im = fac_re[..., None] * b_im + fac_im[..., None] * b_re

    def slab(x):
        return x.reshape((depth, n_slab, GROUPS_PER_SLAB) + x.shape[2:])

    bbd = jnp.concatenate([_block_diag(slab(fb_re).swapaxes(-1, -2)),
                           _block_diag(slab(fb_im).swapaxes(-1, -2))], axis=-1).astype(BF16)
    cbd = jnp.concatenate([_block_diag(slab(c_re).swapaxes(-1, -2)),
                           _block_diag(slab(-c_im).swapaxes(-1, -2))], axis=-2).astype(BF16)

    def lanes(x):
        return x.reshape(depth, n_slab, SLAB_STATES)

    k = jnp.arange(1, seg + 1, dtype=F32)[None, :, None, None]
    pw_mag = jnp.exp(k * (dt * a_re)[:, None])
    pw_arg = k * (dt * a_im)[:, None]
    pw_re, pw_im = pw_mag * jnp.cos(pw_arg), pw_mag * jnp.sin(pw_arg)

    def rows8(x):
        return jnp.broadcast_to(x[..., None, :], x.shape[:-1] + (SUBLANES, x.shape[-1]))

    def seg_lanes(x):
        return x.reshape(depth, seg, n_slab, SLAB_STATES).swapaxes(1, 2)

    ab = rows8(jnp.stack([lanes(abar_re), lanes(abar_im)], axis=2))
    aps = rows8(jnp.stack([lanes(pw_re[:, -1]), lanes(pw_im[:, -1])], axis=2))
    pw = rows8(jnp.stack([seg_lanes(pw_re), seg_lanes(pw_im)], axis=2))
    return dict(bbd=bbd, cbd=cbd, ab=ab, aps=aps, pw=pw,
                d=d[:, None, :], wglu=w_glu.astype(BF16))


def kernel(x_prompt, x_sample, cache_k, cache_v, state_s5_re, state_s5_im, page_table, g_ffn1_pre, g_ffn1_post, w_ffn1_gate, w_ffn1_up, w_ffn1_down, g_mix_pre, g_mix_post, w_in, s5_a_re, s5_a_im, s5_log_dt, s5_b_re, s5_b_im, s5_c_re, s5_c_im, s5_d, w_glu, lambda_q1, lambda_k1, lambda_q2, lambda_k2, g_subln, w_branch_s5, w_branch_att, w_out, g_ffn2_pre, g_ffn2_post, w_ffn2_gate, w_ffn2_up, w_ffn2_down):
    n_batch, seq, d_model = x_prompt.shape
    n_seq = x_sample.shape[0]
    depth = w_in.shape[0]
    n_prompt = n_batch * seq
    groups, n_p = s5_a_re.shape[1:]
    d_s5 = groups * S5_GROUP
    d_qk = N_HEADS * 2 * HEAD_DIM
    d_att = N_HEADS * V_DIM
    n_state = groups * n_p

    xs = (x_prompt.reshape(n_prompt, d_model), x_sample.reshape(n_seq * DEC_SEQ, d_model))

    tc = _largest_tile(seq, S5_CHUNK)
    ffn1 = [w.astype(BF16) for w in (w_ffn1_gate, w_ffn1_up, w_ffn1_down)]
    ffn2 = [w.astype(BF16) for w in (w_ffn2_gate, w_ffn2_up, w_ffn2_down)]
    w_in_b = w_in.astype(BF16)
    wbs_b, wba_b, wo_b = w_branch_s5.astype(BF16), w_branch_att.astype(BF16), w_out.astype(BF16)
    s5p = _s5_params(s5_a_re, s5_a_im, s5_log_dt, s5_b_re, s5_b_im, s5_c_re, s5_c_im,
                     s5_d, w_glu, tc)
    s5p["tc"] = tc
    lam_dyn = (jnp.exp(jnp.sum(lambda_q1 * lambda_k1, axis=-1))
               - jnp.exp(jnp.sum(lambda_q2 * lambda_k2, axis=-1)))
    n_phys = cache_k.shape[1]
    kt_pool = cache_k.transpose(0, 1, 3, 4, 5, 2).reshape(depth, n_phys, d_qk, PAGE_SIZE)
    v_pool = cache_v.reshape(depth, n_phys, PAGE_SIZE * N_HEADS, V_DIM)
    h0_re = state_s5_re.reshape(depth, n_seq, n_state)
    h0_im = state_s5_im.reshape(depth, n_seq, n_state)
    assert w_in.shape[2] == d_s5 + 2 * d_qk + d_att + 2 * d_model == 2 * (2 * d_model)

    def vec(g, l):
        return g[l][None, :]

    stacks = ()
    s5_states = [[] for _ in range(4)]
    for l in range(depth):
        lambda_init = 0.8 - 0.6 * math.exp(-0.3 * l)
        lam = (lam_dyn[l] + lambda_init).reshape(1, 1)

        x = _ffn(xs, vec(g_ffn1_pre, l), vec(g_ffn1_post, l), *ffn1, l, n_prompt, False)
        u, q_f, q_b, k_b, v_b, *stacks = _inproj(
            x, vec(g_mix_pre, l), w_in_b, stacks, l, depth, n_batch, seq, d_s5, d_qk, d_att)
        kt_p, k_s, v_p, v_s = stacks

        y, sp_re, sp_im = _s5_prompt(u, n_batch, seq, s5p, l)
        y, ss_re, ss_im = _s5_sample(u, y, h0_re, h0_im, s5p, l)
        g_sub = vec(g_subln, l)
        o_p, o_s = _attention(page_table, lam, q_b, k_b, v_b, q_f, k_s, v_s, g_sub, kt_pool, v_pool,
                              l, n_batch, seq, 1.0 - lambda_init)
        x = _merge(x, y, o_p, o_s, vec(g_mix_pre, l), w_in_b, wbs_b, wba_b, wo_b,
                   vec(g_mix_post, l), l)
        xs = _ffn((x,), vec(g_ffn2_pre, l), vec(g_ffn2_post, l), *ffn2, l, n_prompt,
                  l == depth - 1)
        xs = xs if l == depth - 1 else (xs,)

        for lst, val in zip(s5_states, (sp_re, sp_im, ss_re, ss_im)):
            lst.append(val)

    sp_re, sp_im, ss_re, ss_im = [jnp.stack(s) for s in s5_states]
    k_p = kt_p.reshape(depth, n_batch, N_HEADS, 2, HEAD_DIM, seq).transpose(0, 1, 5, 2, 3, 4)
    return (xs[0].reshape(n_batch, seq, d_model),
            xs[1].reshape(n_seq, DEC_SEQ, d_model),
            k_p,
            v_p.reshape(depth, n_batch, seq, N_HEADS, V_DIM),
            sp_re.reshape(depth, n_batch, groups, n_p),
            sp_im.reshape(depth, n_batch, groups, n_p),
            k_s.reshape(depth, n_seq, DEC_SEQ, N_HEADS, 2, HEAD_DIM),
            v_s.reshape(depth, n_seq, DEC_SEQ, N_HEADS, V_DIM),
            ss_re.reshape(depth, n_seq, groups, n_p),
            ss_im.reshape(depth, n_seq, groups, n_p))
```

```python
import functools
import math

import jax
import jax.numpy as jnp
from jax import lax
from jax.experimental import pallas as pl
from jax.experimental.pallas import tpu as pltpu

F32 = jnp.float32
BF16 = jnp.bfloat16

PAGE_SIZE = 128
DEC_SEQ = 8
S5_GROUP = 16
S5_STATE = 64
N_HEADS = 4
HEAD_DIM = 64
V_DIM = 2 * HEAD_DIM
RMS_EPS = 1e-6
SUBLN_EPS = 1e-5
NEG_BIG = -1e30
Q_SCALE = HEAD_DIM ** -0.5 * math.log2(math.e)

LANES = 128
SUBLANES = 8
GROUPS_PER_SLAB = LANES // S5_GROUP
SLAB_STATES = GROUPS_PER_SLAB * S5_STATE
FF_CHUNK = 256
VMEM_LIMIT = 56 * 1024 * 1024
FFN_ROWS = 1024
TOKEN_ROWS = 512
ATTN_BLOCK = 512
S5_CHUNK = 512


def _largest_tile(n, pref):
    t = pref
    while n % t:
        t //= 2
    return t


def _rms(x, g, eps):
    return x * lax.rsqrt(jnp.mean(x * x, axis=-1, keepdims=True) + eps) * g


def _cparams(*sem):
    return pltpu.CompilerParams(dimension_semantics=sem, vmem_limit_bytes=VMEM_LIMIT)


def _resident(shape):
    nd = len(shape)
    return pl.BlockSpec(shape, lambda *_: (0,) * nd, pipeline_mode=pl.Buffered(1))


def _layer_resident(stack, layer, block=None, index=None):
    block = tuple(stack.shape[1:]) if block is None else block
    index = (0,) * len(block) if index is None else index
    return pl.BlockSpec((None,) + block, lambda *_: (layer,) + index, pipeline_mode=pl.Buffered(1))


def _ffn_kernel(*refs, n_x, n_out, n_prompt_tiles):
    x_refs = refs[:n_x]
    gpre_ref, gpost_ref, wg_ref, wu_ref, wd_ref = refs[n_x:n_x + 5]
    o_refs = refs[n_x + 5:n_x + 5 + n_out]
    h_scr, acc_scr = refs[n_x + 5 + n_out:]
    is_prompt = pl.program_id(0) < n_prompt_tiles
    x = x_refs[0][...] if n_x == 1 else jnp.where(is_prompt, x_refs[0][...], x_refs[1][...])
    h_scr[...] = _rms(x, gpre_ref[...], RMS_EPS).astype(BF16)
    for c in range(wg_ref.shape[1] // FF_CHUNK):
        cols = slice(c * FF_CHUNK, (c + 1) * FF_CHUNK)
        g = jnp.dot(h_scr[...], wg_ref[:, cols], preferred_element_type=F32)
        u = jnp.dot(h_scr[...], wu_ref[:, cols], preferred_element_type=F32)
        a = (g * jax.nn.sigmoid(g) * u).astype(BF16)
        part = jnp.dot(a, wd_ref[cols, :], preferred_element_type=F32)
        if c == 0:
            acc_scr[...] = part
        else:
            acc_scr[...] += part
    y = x + 0.5 * _rms(acc_scr[...], gpost_ref[...], RMS_EPS)
    if n_out == 1:
        o_refs[0][...] = y
    else:
        @pl.when(is_prompt)
        def _():
            o_refs[0][...] = y

        @pl.when(jnp.logical_not(is_prompt))
        def _():
            o_refs[1][...] = y


def _ffn(xs, gpre, gpost, wg, wu, wd, layer, n_prompt, split_out):
    d = xs[0].shape[1]
    t = sum(x.shape[0] for x in xs)
    tm = _largest_tile(math.gcd(n_prompt, t - n_prompt), FFN_ROWS)
    np_tiles = n_prompt // tm
    row = pl.BlockSpec((tm, d), lambda i: (i, 0))
    p_row = pl.BlockSpec((tm, d), lambda i: (jnp.minimum(i, np_tiles - 1), 0))
    s_row = pl.BlockSpec((tm, d), lambda i: (jnp.maximum(i - np_tiles, 0), 0))
    vec = pl.BlockSpec((1, d), lambda i: (0, 0))
    out_shape = ((jax.ShapeDtypeStruct((n_prompt, d), F32),
                  jax.ShapeDtypeStruct((t - n_prompt, d), F32)) if split_out
                 else jax.ShapeDtypeStruct((t, d), F32))
    return pl.pallas_call(
        functools.partial(_ffn_kernel, n_x=len(xs), n_out=2 if split_out else 1,
                          n_prompt_tiles=np_tiles),
        out_shape=out_shape,
        grid=(t // tm,),
        in_specs=([row] if len(xs) == 1 else [p_row, s_row]) + [vec, vec]
        + [_layer_resident(w, layer) for w in (wg, wu, wd)],
        out_specs=(p_row, s_row) if split_out else row,
        scratch_shapes=[pltpu.VMEM((tm, d), BF16), pltpu.VMEM((tm, d), F32)],
        compiler_params=_cparams("arbitrary"),
        name="ffn",
    )(*xs, gpre, gpost, wg, wu, wd)


def _inproj_kernel(x_ref, g_ref, w_ref, *rest, d_s5, d_qk, d_att, n_prompt_tiles):
    u_ref, qf_ref, qb_ref, kb_ref, vb_ref, ktp_ref, ks_ref, vp_ref, vs_ref = rest[-9:]
    i = pl.program_id(0)
    h = _rms(x_ref[...], g_ref[...], RMS_EPS).astype(BF16)

    def proj(lo, width):
        return jnp.dot(h, w_ref[:, lo:lo + width], preferred_element_type=F32)

    u = proj(0, d_s5)
    for j in range(d_s5 // LANES):
        u_ref[j] = u[:, j * LANES:(j + 1) * LANES]
    q = proj(d_s5, d_qk) * Q_SCALE
    qf_ref[...] = q
    qb_ref[...] = q.astype(BF16)
    k = proj(d_s5 + d_qk, d_qk)
    kb_ref[...] = k.astype(BF16)
    v = proj(d_s5 + 2 * d_qk, d_att)
    vb_ref[...] = v.astype(BF16)

    def store_heads(v_ref):
        for hd in range(N_HEADS):
            v_ref[pl.ds(hd, v.shape[0], stride=N_HEADS), :] = v[:, hd * V_DIM:(hd + 1) * V_DIM]

    @pl.when(i < n_prompt_tiles)
    def _():
        ktp_ref[...] = k.T
        store_heads(vp_ref)

    @pl.when(i >= n_prompt_tiles)
    def _():
        ks_ref[...] = k
        store_heads(vs_ref)


def _inproj(x, g, w, stacks, layer, depth, n_batch, seq, d_s5, d_qk, d_att):
    t, d = x.shape
    tm = _largest_tile(math.gcd(seq, t - n_batch * seq), TOKEN_ROWS)
    np_tiles = n_batch * seq // tm
    per_batch = seq // tm
    rows_s = t - n_batch * seq
    d_proj = d_s5 + 2 * d_qk + d_att

    def rows(width):
        return pl.BlockSpec((tm, width), lambda i: (i, 0))

    def p_tile(i):
        return jnp.minimum(i, np_tiles - 1)

    def s_tile(i):
        return jnp.maximum(i - np_tiles, 0)

    n_slab = d_s5 // LANES
    any_spec = pl.BlockSpec(memory_space=pl.ANY)
    n_in = 3
    return pl.pallas_call(
        functools.partial(_inproj_kernel, d_s5=d_s5, d_qk=d_qk, d_att=d_att,
                          n_prompt_tiles=np_tiles),
        out_shape=(jax.ShapeDtypeStruct((n_slab, t, LANES), F32),
                   jax.ShapeDtypeStruct((t, d_qk), F32),
                   jax.ShapeDtypeStruct((t, d_qk), BF16),
                   jax.ShapeDtypeStruct((t, d_qk), BF16),
                   jax.ShapeDtypeStruct((t, d_att), BF16),
                   jax.ShapeDtypeStruct((depth, n_batch, d_qk, seq), F32),
                   jax.ShapeDtypeStruct((depth, rows_s, d_qk), F32),
                   jax.ShapeDtypeStruct((depth, n_batch * seq * N_HEADS, V_DIM), F32),
                   jax.ShapeDtypeStruct((depth, rows_s * N_HEADS, V_DIM), F32)),
        grid=(t // tm,),
        in_specs=[rows(d), pl.BlockSpec((1, d), lambda i: (0, 0)),
                  _layer_resident(w, layer, block=(d, d_proj))] + [any_spec] * len(stacks),
        out_specs=(pl.BlockSpec((n_slab, tm, LANES), lambda i: (0, i, 0)),
                   rows(d_qk), rows(d_qk), rows(d_qk), rows(d_att),
                   pl.BlockSpec((None, None, d_qk, tm),
                                lambda i: (layer, p_tile(i) // per_batch, 0, p_tile(i) % per_batch)),
                   pl.BlockSpec((None, tm, d_qk), lambda i: (layer, s_tile(i), 0)),
                   pl.BlockSpec((None, tm * N_HEADS, V_DIM), lambda i: (layer, p_tile(i), 0)),
                   pl.BlockSpec((None, tm * N_HEADS, V_DIM), lambda i: (layer, s_tile(i), 0))),
        input_output_aliases={n_in + s: 5 + s for s in range(len(stacks))},
        compiler_params=_cparams("arbitrary"),
        name="in_proj",
    )(x, g, w, *stacks)


def _s5_readout(st_scr, up_scr, cbd_ref, d_ref, wglu_ref, n_slab):
    ys = []
    for j in range(n_slab):
        hj = st_scr[:, 2 * SLAB_STATES * j:2 * SLAB_STATES * (j + 1)].astype(BF16)
        ys.append(jnp.dot(hj, cbd_ref[j], preferred_element_type=F32))
    y = jnp.concatenate(ys, axis=-1) + d_ref[...] * up_scr[...]
    z = jax.nn.gelu(y).astype(BF16)
    zg = jnp.dot(z, wglu_ref[...], preferred_element_type=F32)
    half = zg.shape[1] // 2
    return zg[:, :half] * jax.nn.sigmoid(zg[:, half:])


def _s5_input(st_scr, up_scr, bbd_ref, n_slab):
    for j in range(n_slab):
        uj = up_scr[:, j * LANES:(j + 1) * LANES].astype(BF16)
        st_scr[:, 2 * SLAB_STATES * j:2 * SLAB_STATES * (j + 1)] = jnp.dot(
            uj, bbd_ref[j], preferred_element_type=F32)


def _s5_prompt_kernel(u_ref, bbd_ref, cbd_ref, ab_ref, aps_ref, pw_ref, d_ref, wglu_ref,
                      y_ref, sre_ref, sim_ref, up_scr, st_scr, carry_scr, *, tc):
    c = pl.program_id(1)
    n_slab = u_ref.shape[0]
    seg = tc // SUBLANES
    w = SLAB_STATES

    @pl.when(c == 0)
    def _():
        carry_scr[...] = jnp.zeros_like(carry_scr)

    for j in range(n_slab):
        for k in range(seg):
            up_scr[k * SUBLANES:(k + 1) * SUBLANES, j * LANES:(j + 1) * LANES] = (
                u_ref[j, pl.ds(k, SUBLANES, stride=seg), :])
    _s5_input(st_scr, up_scr, bbd_ref, n_slab)

    row_id = lax.broadcasted_iota(jnp.int32, (SUBLANES, w), 0)
    for j in range(n_slab):
        re = slice(2 * w * j, 2 * w * j + w)
        im = slice(2 * w * j + w, 2 * w * (j + 1))
        a_re = ab_ref[j, 0]
        a_im = ab_ref[j, 1]

        def scan_step(k, h, re=re, im=im, a_re=a_re, a_im=a_im):
            h_re, h_im = h
            r = pl.ds(pl.multiple_of(k * SUBLANES, SUBLANES), SUBLANES)
            n_re = a_re * h_re - a_im * h_im + st_scr[r, re]
            n_im = a_re * h_im + a_im * h_re + st_scr[r, im]
            st_scr[r, re] = n_re
            st_scr[r, im] = n_im
            return n_re, n_im

        zero = jnp.zeros((SUBLANES, w), F32)
        e_re, e_im = lax.fori_loop(0, seg, scan_step, (zero, zero), unroll=True)

        s_re = aps_ref[j, 0][0:1]
        s_im = aps_ref[j, 1][0:1]
        cur_re = carry_scr[0:1, re]
        cur_im = carry_scr[0:1, im]
        c_re = jnp.broadcast_to(cur_re, (SUBLANES, w))
        c_im = jnp.broadcast_to(cur_im, (SUBLANES, w))
        for i in range(1, SUBLANES + 1):
            nxt_re = s_re * cur_re - s_im * cur_im + e_re[i - 1:i]
            nxt_im = s_re * cur_im + s_im * cur_re + e_im[i - 1:i]
            cur_re, cur_im = nxt_re, nxt_im
            if i < SUBLANES:
                c_re = jnp.where(row_id == i, cur_re, c_re)
                c_im = jnp.where(row_id == i, cur_im, c_im)
        carry_scr[0:1, re] = cur_re
        carry_scr[0:1, im] = cur_im

        def fix_step(k, carry, re=re, im=im, c_re=c_re, c_im=c_im, j=j):
            r = pl.ds(pl.multiple_of(k * SUBLANES, SUBLANES), SUBLANES)
            p_re = pw_ref[j, 0, k]
            p_im = pw_ref[j, 1, k]
            st_scr[r, re] += p_re * c_re - p_im * c_im
            st_scr[r, im] += p_re * c_im + p_im * c_re
            return carry

        lax.fori_loop(0, seg, fix_step, 0, unroll=True)

    up_scr[...] = _s5_readout(st_scr, up_scr, cbd_ref, d_ref, wglu_ref, n_slab)
    for j in range(n_slab):
        for k in range(seg):
            y_ref[j, pl.ds(k, SUBLANES, stride=seg), :] = (
                up_scr[k * SUBLANES:(k + 1) * SUBLANES, j * LANES:(j + 1) * LANES])

    @pl.when(c == pl.num_programs(1) - 1)
    def _():
        for j in range(n_slab):
            sre_ref[0, :, j * w:(j + 1) * w] = carry_scr[0:1, 2 * w * j:2 * w * j + w]
            sim_ref[0, :, j * w:(j + 1) * w] = carry_scr[0:1, 2 * w * j + w:2 * w * (j + 1)]


def _s5_prompt(u_slab, n_batch, seq, p, layer):
    n_slab = u_slab.shape[0]
    d_s5 = n_slab * LANES
    tc = p["tc"]
    n_chunks = seq // tc
    n_state = n_slab * SLAB_STATES
    u_spec = pl.BlockSpec((n_slab, tc, LANES), lambda b, c: (0, b * n_chunks + c, 0))
    st_spec = pl.BlockSpec((1, 1, n_state), lambda b, c: (b, 0, 0))
    return pl.pallas_call(
        functools.partial(_s5_prompt_kernel, tc=tc),
        out_shape=(jax.ShapeDtypeStruct((n_slab, n_batch * seq, LANES), F32),
                   jax.ShapeDtypeStruct((n_batch, 1, n_state), F32),
                   jax.ShapeDtypeStruct((n_batch, 1, n_state), F32)),
        grid=(n_batch, n_chunks),
        in_specs=[u_spec] + [_layer_resident(p[k], layer)
                             for k in ("bbd", "cbd", "ab", "aps", "pw", "d", "wglu")],
        out_specs=(u_spec, st_spec, st_spec),
        scratch_shapes=[pltpu.VMEM((tc, d_s5), F32),
                        pltpu.VMEM((tc, 2 * n_state), F32),
                        pltpu.VMEM((SUBLANES, 2 * n_state), F32)],
        compiler_params=_cparams("arbitrary", "arbitrary"),
        name="s5_prompt",
    )(u_slab, p["bbd"], p["cbd"], p["ab"], p["aps"], p["pw"], p["d"], p["wglu"])


def _s5_sample_kernel(u_ref, h0re_ref, h0im_ref, bbd_ref, cbd_ref, ab_ref, d_ref, wglu_ref,
                      y_ref, sre_ref, sim_ref, up_scr, st_scr):
    n_slab = u_ref.shape[0]
    n_seq = h0re_ref.shape[0]
    w = SLAB_STATES
    for j in range(n_slab):
        for t in range(DEC_SEQ):
            up_scr[t * n_seq:(t + 1) * n_seq, j * LANES:(j + 1) * LANES] = (
                u_ref[j, pl.ds(t, n_seq, stride=DEC_SEQ), :])
    _s5_input(st_scr, up_scr, bbd_ref, n_slab)

    for j in range(n_slab):
        re = slice(2 * w * j, 2 * w * j + w)
        im = slice(2 * w * j + w, 2 * w * (j + 1))
        a_re = ab_ref[j, 0]
        a_im = ab_ref[j, 1]

        def seq_group(g, carry, re=re, im=im, a_re=a_re, a_im=a_im, j=j):
            n0 = pl.multiple_of(g * SUBLANES, SUBLANES)
            h_re = h0re_ref[pl.ds(n0, SUBLANES), j * w:(j + 1) * w]
            h_im = h0im_ref[pl.ds(n0, SUBLANES), j * w:(j + 1) * w]
            for t in range(DEC_SEQ):
                r = pl.ds(pl.multiple_of(t * n_seq + n0, SUBLANES), SUBLANES)
                n_re = a_re * h_re - a_im * h_im + st_scr[r, re]
                n_im = a_re * h_im + a_im * h_re + st_scr[r, im]
                st_scr[r, re] = n_re
                st_scr[r, im] = n_im
                h_re, h_im = n_re, n_im
            sre_ref[pl.ds(n0, SUBLANES), j * w:(j + 1) * w] = h_re
            sim_ref[pl.ds(n0, SUBLANES), j * w:(j + 1) * w] = h_im
            return carry

        lax.fori_loop(0, n_seq // SUBLANES, seq_group, 0)

    up_scr[...] = _s5_readout(st_scr, up_scr, cbd_ref, d_ref, wglu_ref, n_slab)
    for j in range(n_slab):
        for t in range(DEC_SEQ):
            y_ref[j, pl.ds(t, n_seq, stride=DEC_SEQ), :] = (
                up_scr[t * n_seq:(t + 1) * n_seq, j * LANES:(j + 1) * LANES])


def _s5_sample(u_slab, h0_re, h0_im, p, layer):
    n_slab, t, _ = u_slab.shape
    n_seq, n_state = h0_re.shape[1:]
    rows = n_seq * DEC_SEQ
    tail = pl.BlockSpec((n_slab, rows, LANES), lambda i: (0, (t - rows) // rows, 0))
    params = (h0_re, h0_im, p["bbd"], p["cbd"], p["ab"], p["d"], p["wglu"])
    return pl.pallas_call(
        _s5_sample_kernel,
        out_shape=(jax.ShapeDtypeStruct((n_slab, rows, LANES), F32),
                   jax.ShapeDtypeStruct((n_seq, n_state), F32),
                   jax.ShapeDtypeStruct((n_seq, n_state), F32)),
        grid=(1,),
        in_specs=[tail] + [_layer_resident(a, layer) for a in params],
        out_specs=(_resident((n_slab, rows, LANES)), _resident((n_seq, n_state)),
                   _resident((n_seq, n_state))),
        scratch_shapes=[pltpu.VMEM((rows, n_slab * LANES), F32),
                        pltpu.VMEM((rows, 2 * n_state), F32)],
        compiler_params=_cparams("arbitrary"),
        name="s5_sample",
    )(u_slab, *params)


def _subln(o, g_ref, post_scale):
    return _rms(o, g_ref[...], SUBLN_EPS) * post_scale


def _prompt_attention(lam, qi, q_ref, k_ref, v_ref, g_ref, o_ref, q2_scr, m_scr, l_scr, acc_scr,
                      blk, post_scale):
    q = q_ref[...]
    lane = lax.broadcasted_iota(jnp.int32, q.shape, 1)
    zero = jnp.zeros_like(q)
    q2_scr[0:blk] = jnp.where(lane < HEAD_DIM, q, zero)
    q2_scr[blk:2 * blk] = jnp.where(lane >= HEAD_DIM, q, zero)
    m_scr[...] = jnp.full_like(m_scr, NEG_BIG)
    l_scr[...] = jnp.zeros_like(l_scr)
    acc_scr[...] = jnp.zeros_like(acc_scr)

    n_rep = blk // LANES

    def block(kj, masked):
        r = pl.ds(pl.multiple_of(kj * blk, blk), blk)
        s = lax.dot_general(q2_scr[...], k_ref[r, :], (((1,), (1,)), ((), ())),
                            preferred_element_type=F32)
        if masked:
            qpos = lax.broadcasted_iota(jnp.int32, s.shape, 0) % blk
            kpos = lax.broadcasted_iota(jnp.int32, s.shape, 1)
            s = jnp.where(kpos <= qpos, s, NEG_BIG)
        m_old = m_scr[...]
        m_new = jnp.maximum(m_old, jnp.max(s, axis=-1, keepdims=True))
        alpha = jnp.exp2(m_old - m_new)
        p = jnp.exp2(s - jnp.concatenate([m_new] * n_rep, axis=-1))
        l_scr[...] = alpha * l_scr[...] + jnp.sum(p, axis=-1, keepdims=True)
        acc_scr[...] = alpha * acc_scr[...] + jnp.dot(p.astype(BF16), v_ref[r, :],
                                                      preferred_element_type=F32)
        m_scr[...] = m_new

    def full_block(kj, carry):
        block(kj, False)
        return carry

    lax.fori_loop(0, qi, full_block, 0)
    block(qi, True)

    o1 = acc_scr[0:blk] / l_scr[0:blk]
    o2 = acc_scr[blk:2 * blk] / l_scr[blk:2 * blk]
    o_ref[...] = _subln(o1 - lam * o2, g_ref, post_scale)


def _sample_attention(lam, q_ref, kn_ref, vn_ref, g_ref, kt_pages, v_pages, o_ref, kt_scr, v_scr,
                      post_scale):
    n_row = N_HEADS * 2 * DEC_SEQ

    for j in range(len(kt_pages)):
        cols = slice(j * PAGE_SIZE, (j + 1) * PAGE_SIZE)
        kt_scr[:, cols] = kt_pages[j][...].astype(BF16)
        for h in range(N_HEADS):
            v_scr[h, cols, :] = v_pages[j][pl.ds(h, PAGE_SIZE, stride=N_HEADS), :].astype(BF16)
    def pad_rows(x):
        pad = jnp.zeros((PAGE_SIZE - DEC_SEQ, x.shape[1]), F32)
        return jnp.concatenate([x, pad], axis=0).astype(BF16)

    k_new = pad_rows(kn_ref[...])

    qt = jnp.concatenate([q_ref[...]] * (n_row // DEC_SEQ), axis=0)
    rid = lax.broadcasted_iota(jnp.int32, qt.shape, 0)
    blk_id = lax.broadcasted_iota(jnp.int32, qt.shape, 1) // HEAD_DIM
    qm = jnp.where(blk_id == rid // DEC_SEQ, qt, 0.0).astype(BF16)

    s = jnp.dot(qm, kt_scr[...], preferred_element_type=F32)
    s_new = lax.dot_general(qm, k_new, (((1,), (1,)), ((), ())), preferred_element_type=F32)
    t_new = lax.broadcasted_iota(jnp.int32, s_new.shape, 1)
    q_idx = lax.broadcasted_iota(jnp.int32, s_new.shape, 0) % DEC_SEQ
    s_new = jnp.where(t_new <= q_idx, s_new, NEG_BIG)

    m = jnp.maximum(jnp.max(s, axis=-1, keepdims=True), jnp.max(s_new, axis=-1, keepdims=True))
    p = jnp.exp2(s - m)
    p_new = jnp.exp2(s_new - m)
    l = jnp.sum(p, axis=-1, keepdims=True) + jnp.sum(p_new, axis=-1, keepdims=True)
    pb = p.astype(BF16)
    pb_new = p_new.astype(BF16)
    for h in range(N_HEADS):
        rows = slice(2 * DEC_SEQ * h, 2 * DEC_SEQ * (h + 1))
        v_new = pad_rows(vn_ref[pl.ds(h, DEC_SEQ, stride=N_HEADS), :])
        o_h = (jnp.dot(pb[rows], v_scr[h], preferred_element_type=F32)
               + jnp.dot(pb_new[rows], v_new, preferred_element_type=F32)) / l[rows]
        o_ref[:, h * V_DIM:(h + 1) * V_DIM] = _subln(
            o_h[0:DEC_SEQ] - lam * o_h[DEC_SEQ:2 * DEC_SEQ], g_ref, post_scale)


def _attn_kernel(pt_ref, lam_ref, q_ref, k_ref, v_ref, g_ref, qs_ref, kn_ref, vn_ref, *rest,
                 blk, n_pages, n_seq, post_scale):
    del pt_ref
    kt_pages = rest[:n_pages]
    v_pages = rest[n_pages:2 * n_pages]
    o_ref, os_ref = rest[2 * n_pages:2 * n_pages + 2]
    q2_scr, m_scr, l_scr, acc_scr, kt_scr, v_scr = rest[2 * n_pages + 2:]
    lam = lam_ref[0, 0]
    qi = pl.program_id(2)
    step = (pl.program_id(0) * pl.num_programs(1) + pl.program_id(1)) * pl.num_programs(2) + qi
    _prompt_attention(lam, qi, q_ref, k_ref, v_ref, g_ref, o_ref, q2_scr, m_scr, l_scr, acc_scr,
                      blk, post_scale)

    @pl.when(step < n_seq)
    def _():
        _sample_attention(lam, qs_ref, kn_ref, vn_ref, g_ref, kt_pages, v_pages, os_ref,
                          kt_scr, v_scr, post_scale)


def _attention(page_table, lam, q_b, k_b, v_b, q_f, k_s, v_s, g_subln, kt_pool, v_pool, layer,
               n_batch, seq, post_scale):
    n_seq, n_pages = page_table.shape
    t, width = q_f.shape
    past = n_pages * PAGE_SIZE
    blk = _largest_tile(seq, ATTN_BLOCK)
    nq = seq // blk
    assert n_seq <= n_batch * N_HEADS * nq, "one sample sequence per grid step"
    first = t // DEC_SEQ - n_seq

    def sample(b, h, i):
        return jnp.minimum((b * N_HEADS + h) * nq + i, n_seq - 1)

    q_spec = pl.BlockSpec((blk, V_DIM), lambda b, h, i, pt: (b * nq + i, h))
    kv_spec = pl.BlockSpec((seq, V_DIM), lambda b, h, i, pt: (b, h))
    tok = pl.BlockSpec((DEC_SEQ, width), lambda b, h, i, pt: (first + sample(b, h, i), 0))
    new_k = pl.BlockSpec((None, DEC_SEQ, width), lambda b, h, i, pt: (layer, sample(b, h, i), 0))
    new_v = pl.BlockSpec((None, DEC_SEQ * N_HEADS, V_DIM),
                         lambda b, h, i, pt: (layer, sample(b, h, i), 0))

    def page_spec(shape, j):
        return pl.BlockSpec((None, None) + shape,
                            lambda b, h, i, pt, j=j: (layer, pt[sample(b, h, i), j], 0, 0))

    kt_specs = [page_spec(kt_pool.shape[2:], j) for j in range(n_pages)]
    v_specs = [page_spec(v_pool.shape[2:], j) for j in range(n_pages)]
    grid_spec = pltpu.PrefetchScalarGridSpec(
        num_scalar_prefetch=1,
        grid=(n_batch, N_HEADS, nq),
        in_specs=[pl.BlockSpec(memory_space=pltpu.SMEM), q_spec, kv_spec, kv_spec,
                  pl.BlockSpec((1, V_DIM), lambda b, h, i, pt: (0, 0)),
                  tok, new_k, new_v] + kt_specs + v_specs,
        out_specs=(q_spec,
                   pl.BlockSpec((DEC_SEQ, width), lambda b, h, i, pt: (sample(b, h, i), 0))),
        scratch_shapes=[pltpu.VMEM((2 * blk, V_DIM), BF16),
                        pltpu.VMEM((2 * blk, LANES), F32),
                        pltpu.VMEM((2 * blk, LANES), F32),
                        pltpu.VMEM((2 * blk, V_DIM), F32),
                        pltpu.VMEM((width, past), BF16),
                        pltpu.VMEM((N_HEADS, past, V_DIM), BF16)],
    )
    return pl.pallas_call(
        functools.partial(_attn_kernel, blk=blk, n_pages=n_pages, n_seq=n_seq,
                          post_scale=post_scale),
        out_shape=(jax.ShapeDtypeStruct((n_batch * seq, N_HEADS * V_DIM), F32),
                   jax.ShapeDtypeStruct((n_seq * DEC_SEQ, width), F32)),
        grid_spec=grid_spec,
        compiler_params=_cparams("arbitrary", "arbitrary", "arbitrary"),
        name="attention",
    )(page_table, lam, q_b, k_b, v_b, g_subln, q_f, k_s, v_s,
      *([kt_pool] * n_pages), *([v_pool] * n_pages))


def _merge_kernel(x_ref, ysp_ref, yss_ref, yap_ref, yas_ref, gpre_ref, wgate_ref, wbs_ref, wba_ref,
                  wo_ref, g_ref, o_ref, *, n_prompt_tiles):
    d = x_ref.shape[1]
    is_prompt = pl.program_id(0) < n_prompt_tiles

    def slabs(ref):
        return jnp.concatenate([ref[j] for j in range(ref.shape[0])], axis=-1)

    ys = jnp.where(is_prompt, slabs(ysp_ref), slabs(yss_ref)).astype(BF16)
    ya = jnp.where(is_prompt, yap_ref[...], yas_ref[...]).astype(BF16)
    h = _rms(x_ref[...], gpre_ref[...], RMS_EPS).astype(BF16)
    gate = jnp.dot(h, wgate_ref[...], preferred_element_type=F32)
    merged = (jax.nn.sigmoid(gate[:, :d]) * jnp.dot(ys, wbs_ref[...], preferred_element_type=F32)
              + jax.nn.sigmoid(gate[:, d:]) * jnp.dot(ya, wba_ref[...], preferred_element_type=F32))
    out = jnp.dot(merged.astype(BF16), wo_ref[...], preferred_element_type=F32)
    o_ref[...] = x_ref[...] + _rms(out, g_ref[...], RMS_EPS)


def _merge(x, y_s5_p, y_s5_s, y_att_p, y_att_s, g_pre, w_in, wbs, wba, wo, g_post, layer):
    t, d = x.shape
    tm = _largest_tile(math.gcd(y_att_p.shape[0], y_att_s.shape[0]), TOKEN_ROWS)
    np_tiles = y_att_p.shape[0] // tm
    n_slab = y_s5_p.shape[0]
    width = y_att_p.shape[1]

    def rows(width):
        return pl.BlockSpec((tm, width), lambda i: (i, 0))

    vec = pl.BlockSpec((1, d), lambda i: (0, 0))
    return pl.pallas_call(
        functools.partial(_merge_kernel, n_prompt_tiles=np_tiles),
        out_shape=jax.ShapeDtypeStruct((t, d), F32),
        grid=(t // tm,),
        in_specs=[rows(d),
                  pl.BlockSpec((n_slab, tm, LANES), lambda i: (0, jnp.minimum(i, np_tiles - 1), 0)),
                  pl.BlockSpec((n_slab, tm, LANES), lambda i: (0, jnp.maximum(i - np_tiles, 0), 0)),
                  pl.BlockSpec((tm, width), lambda i: (jnp.minimum(i, np_tiles - 1), 0)),
                  pl.BlockSpec((tm, width), lambda i: (jnp.maximum(i - np_tiles, 0), 0)), vec,
                  _layer_resident(w_in, layer, block=(d, 2 * d), index=(0, w_in.shape[2] // (2 * d) - 1)),
                  _layer_resident(wbs, layer), _layer_resident(wba, layer),
                  _layer_resident(wo, layer), vec],
        out_specs=rows(d),
        compiler_params=_cparams("arbitrary"),
        name="merge",
    )(x, y_s5_p, y_s5_s, y_att_p, y_att_s, g_pre, w_in, wbs, wba, wo, g_post)


def _block_diag(m):
    eye = jnp.eye(GROUPS_PER_SLAB, dtype=m.dtype)
    depth, n_slab, g, r, c = m.shape
    return jnp.einsum("ljgrc,gh->ljgrhc", m, eye).reshape(depth, n_slab, g * r, g * c)


def _s5_params(a_re, a_im, log_dt, b_re, b_im, c_re, c_im, d, w_glu, tc):
    depth, groups, n_p = a_re.shape
    n_slab = groups // GROUPS_PER_SLAB
    seg = tc // SUBLANES
    dt = jnp.exp(log_dt)[:, :, None]
    mag = jnp.exp(dt * a_re)
    abar_re, abar_im = mag * jnp.cos(dt * a_im), mag * jnp.sin(dt * a_im)
    den = a_re * a_re + a_im * a_im
    inv_re, inv_im = a_re / den, -a_im / den
    fac_re = (abar_re - 1.0) * inv_re - abar_im * inv_im
    fac_im = (abar_re - 1.0) * inv_im + abar_im * inv_re
    fb_re = fac_re[..., None] * b_re - fac_im[..., None] * b_im
    fb_im = fac_re[..., None] * b_im + fac_im[..., None] * b_re

    def slab(x):
        return x.reshape((depth, n_slab, GROUPS_PER_SLAB) + x.shape[2:])

    bbd = jnp.concatenate([_block_diag(slab(fb_re).swapaxes(-1, -2)),
                           _block_diag(slab(fb_im).swapaxes(-1, -2))], axis=-1).astype(BF16)
    cbd = jnp.concatenate([_block_diag(slab(c_re).swapaxes(-1, -2)),
                           _block_diag(slab(-c_im).swapaxes(-1, -2))], axis=-2).astype(BF16)

    def lanes(x):
        return x.reshape(depth, n_slab, SLAB_STATES)

    k = jnp.arange(1, seg + 1, dtype=F32)[None, :, None, None]
    pw_mag = jnp.exp(k * (dt * a_re)[:, None])
    pw_arg = k * (dt * a_im)[:, None]
    pw_re, pw_im = pw_mag * jnp.cos(pw_arg), pw_mag * jnp.sin(pw_arg)

    def rows8(x):
        return jnp.broadcast_to(x[..., None, :], x.shape[:-1] + (SUBLANES, x.shape[-1]))

    def seg_lanes(x):
        return x.reshape(depth, seg, n_slab, SLAB_STATES).swapaxes(1, 2)

    ab = rows8(jnp.stack([lanes(abar_re), lanes(abar_im)], axis=2))
    aps = rows8(jnp.stack([lanes(pw_re[:, -1]), lanes(pw_im[:, -1])], axis=2))
    pw = rows8(jnp.stack([seg_lanes(pw_re), seg_lanes(pw_im)], axis=2))
    return dict(bbd=bbd, cbd=cbd, ab=ab, aps=aps, pw=pw,
                d=d[:, None, :], wglu=w_glu.astype(BF16))


def kernel(x_prompt, x_sample, cache_k, cache_v, state_s5_re, state_s5_im, page_table, g_ffn1_pre, g_ffn1_post, w_ffn1_gate, w_ffn1_up, w_ffn1_down, g_mix_pre, g_mix_post, w_in, s5_a_re, s5_a_im, s5_log_dt, s5_b_re, s5_b_im, s5_c_re, s5_c_im, s5_d, w_glu, lambda_q1, lambda_k1, lambda_q2, lambda_k2, g_subln, w_branch_s5, w_branch_att, w_out, g_ffn2_pre, g_ffn2_post, w_ffn2_gate, w_ffn2_up, w_ffn2_down):
    n_batch, seq, d_model = x_prompt.shape
    n_seq = x_sample.shape[0]
    depth = w_in.shape[0]
    n_prompt = n_batch * seq
    groups, n_p = s5_a_re.shape[1:]
    d_s5 = groups * S5_GROUP
    d_qk = N_HEADS * 2 * HEAD_DIM
    d_att = N_HEADS * V_DIM
    n_state = groups * n_p

    xs = (x_prompt.reshape(n_prompt, d_model), x_sample.reshape(n_seq * DEC_SEQ, d_model))

    tc = _largest_tile(seq, S5_CHUNK)
    ffn1 = [w.astype(BF16) for w in (w_ffn1_gate, w_ffn1_up, w_ffn1_down)]
    ffn2 = [w.astype(BF16) for w in (w_ffn2_gate, w_ffn2_up, w_ffn2_down)]
    w_in_b = w_in.astype(BF16)
    wbs_b, wba_b, wo_b = w_branch_s5.astype(BF16), w_branch_att.astype(BF16), w_out.astype(BF16)
    s5p = _s5_params(s5_a_re, s5_a_im, s5_log_dt, s5_b_re, s5_b_im, s5_c_re, s5_c_im,
                     s5_d, w_glu, tc)
    s5p["tc"] = tc
    lam_dyn = (jnp.exp(jnp.sum(lambda_q1 * lambda_k1, axis=-1))
               - jnp.exp(jnp.sum(lambda_q2 * lambda_k2, axis=-1)))
    n_phys = cache_k.shape[1]
    kt_pool = cache_k.transpose(0, 1, 3, 4, 5, 2).reshape(depth, n_phys, d_qk, PAGE_SIZE)
    v_pool = cache_v.reshape(depth, n_phys, PAGE_SIZE * N_HEADS, V_DIM)
    h0_re = state_s5_re.reshape(depth, n_seq, n_state)
    h0_im = state_s5_im.reshape(depth, n_seq, n_state)
    assert w_in.shape[2] == d_s5 + 2 * d_qk + d_att + 2 * d_model == 2 * (2 * d_model)

    def vec(g, l):
        return g[l][None, :]

    rows_s = n_seq * DEC_SEQ
    stacks = (jnp.zeros((depth, n_batch, d_qk, seq), F32),
              jnp.zeros((depth, rows_s, d_qk), F32),
              jnp.zeros((depth, n_prompt * N_HEADS, V_DIM), F32),
              jnp.zeros((depth, rows_s * N_HEADS, V_DIM), F32))
    s5_states = [[] for _ in range(4)]
    for l in range(depth):
        lambda_init = 0.8 - 0.6 * math.exp(-0.3 * l)
        lam = (lam_dyn[l] + lambda_init).reshape(1, 1)

        x = _ffn(xs, vec(g_ffn1_pre, l), vec(g_ffn1_post, l), *ffn1, l, n_prompt, False)
        u, q_f, q_b, k_b, v_b, *stacks = _inproj(
            x, vec(g_mix_pre, l), w_in_b, stacks, l, depth, n_batch, seq, d_s5, d_qk, d_att)
        kt_p, k_s, v_p, v_s = stacks

        y_p, sp_re, sp_im = _s5_prompt(u, n_batch, seq, s5p, l)
        y_s, ss_re, ss_im = _s5_sample(u, h0_re, h0_im, s5p, l)
        g_sub = vec(g_subln, l)
        o_p, o_s = _attention(page_table, lam, q_b, k_b, v_b, q_f, k_s, v_s, g_sub, kt_pool, v_pool,
                              l, n_batch, seq, 1.0 - lambda_init)
        x = _merge(x, y_p, y_s, o_p, o_s, vec(g_mix_pre, l), w_in_b, wbs_b, wba_b, wo_b,
                   vec(g_mix_post, l), l)
        xs = _ffn((x,), vec(g_ffn2_pre, l), vec(g_ffn2_post, l), *ffn2, l, n_prompt,
                  l == depth - 1)
        xs = xs if l == depth - 1 else (xs,)

        for lst, val in zip(s5_states, (sp_re, sp_im, ss_re, ss_im)):
            lst.append(val)

    sp_re, sp_im, ss_re, ss_im = [jnp.stack(s) for s in s5_states]
    k_p = kt_p.reshape(depth, n_batch, N_HEADS, 2, HEAD_DIM, seq).transpose(0, 1, 5, 2, 3, 4)
    return (xs[0].reshape(n_batch, seq, d_model),
            xs[1].reshape(n_seq, DEC_SEQ, d_model),
            k_p,
            v_p.reshape(depth, n_batch, seq, N_HEADS, V_DIM),
            sp_re.reshape(depth, n_batch, groups, n_p),
            sp_im.reshape(depth, n_batch, groups, n_p),
            k_s.reshape(depth, n_seq, DEC_SEQ, N_HEADS, 2, HEAD_DIM),
            v_s.reshape(depth, n_seq, DEC_SEQ, N_HEADS, V_DIM),
            ss_re.reshape(depth, n_seq, groups, n_p),
            ss_im.reshape(depth, n_seq, groups, n_p))
```
